```python
import math
import jax, jax.numpy as jnp
from jax import lax
import numpy as np

D_MODEL = 1024
BATCH = 16
SEQ = 2048
DEPTH = 4
DEC_BATCH = 128
DEC_SEQ = 1
PAST_LEN = 8192
PAGE_SIZE = 128

MLA_HEADS = 8
Q_LORA = 384
KV_LORA = 256
QK_NOPE = 64
QK_ROPE = 32
QK_HEAD = QK_NOPE + QK_ROPE
V_HEAD = 64
ROPE_BASE = 10000.0
Q_BLOCK = 128
NEG_INF = -1e30
D_CONV = 384
CONV_WIDTH = 3
D_POOL = 384
POOL_WINDOWS = (2, 4, 8, 16)
N_POOL_GROUPS = 4
POOL_GROUP = D_POOL // N_POOL_GROUPS
POOL_BUF = 15
D_SSM = 384
SSM_GROUP = 16
N_SSM_GROUPS = D_SSM // SSM_GROUP
SSM_STATE = 64
DT_MIN = 1e-3
DT_MAX = 1e-1
N_BRANCHES = 4
D_FF = 2816
N_EXPERTS = 8
TOP_K = 2
D_FF_EXPERT = 1408
N_DENSE = (DEPTH + 1) // 2
N_MOE = DEPTH // 2
RMS_EPS = 1e-6
IN_SIZES = (Q_LORA, KV_LORA, QK_ROPE, D_CONV, D_CONV, D_CONV, D_POOL, D_SSM, N_BRANCHES * D_MODEL)
D_IN = Q_LORA + KV_LORA + QK_ROPE + 3 * D_CONV + D_POOL + D_SSM + N_BRANCHES * D_MODEL

kernel_name = 'hybrid_gated_mla_conv_pool_s5_moe_step'


def _split_points():
    pts, acc = [], 0
    for s in IN_SIZES[:-1]:
        acc += s
        pts.append(acc)
    return pts


def _rmsnorm(x, g):
    xf = x.astype(jnp.float32)
    y = xf * lax.rsqrt(jnp.mean(xf * xf, axis=-1, keepdims=True) + RMS_EPS)
    return (y * g.astype(jnp.float32)).astype(x.dtype)


def _rope_tables(pos):
    inv_freq = jnp.power(ROPE_BASE, -jnp.arange(0, QK_ROPE, 2, dtype=jnp.float32) / QK_ROPE)
    ang = pos.astype(jnp.float32)[:, None] * inv_freq[None, :]
    return jnp.cos(ang), jnp.sin(ang)


def _apply_rope(x, cos, sin):
    half = x.shape[-1] // 2
    x1, x2 = x[..., :half], x[..., half:]
    cos = cos.astype(x.dtype)
    sin = sin.astype(x.dtype)
    return jnp.concatenate([x1 * cos - x2 * sin, x1 * sin + x2 * cos], axis=-1)


def _mla_keys_values(ckv, kpe, w_kv_up, k_norm_g):
    kv = jnp.einsum('...tc,chd->...thd', ckv, w_kv_up.reshape(KV_LORA, MLA_HEADS, QK_NOPE + V_HEAD))
    k_nope, v = kv[..., :QK_NOPE], kv[..., QK_NOPE:]
    k_pe = jnp.broadcast_to(kpe[..., None, :], k_nope.shape[:-1] + (QK_ROPE,)).astype(k_nope.dtype)
    k = jnp.concatenate([k_nope, k_pe], axis=-1)
    return _rmsnorm(k, k_norm_g), v


def _attend(q, k, v, mask):
    s = jnp.einsum('bqhd,bkhd->bhqk', q, k).astype(jnp.float32) * (QK_HEAD ** -0.5)
    s = jnp.where(mask[None, None], s, NEG_INF)
    p = jax.nn.softmax(s, axis=-1).astype(v.dtype)
    return jnp.einsum('bhqk,bkhd->bqhd', p, v)


def _causal_block_attention(q, k, v):
    n, S = q.shape[0], q.shape[1]
    kpos = jnp.arange(S)

    def block(i):
        start = i * Q_BLOCK
        q_blk = lax.dynamic_slice_in_dim(q, start, Q_BLOCK, axis=1)
        qpos = start + jnp.arange(Q_BLOCK)
        return _attend(q_blk, k, v, kpos[None, :] <= qpos[:, None])

    out = lax.map(block, jnp.arange(S // Q_BLOCK))
    return jnp.moveaxis(out, 0, 1).reshape(n, S, MLA_HEADS, V_HEAD)


def _paged_attention(q, ckv_new, kpe_new, cache_ckv, cache_kpe, page_table, layer, w_kv_up, k_norm_g):
    n_new = q.shape[1]
    past = page_table.shape[1] * PAGE_SIZE
    new_mask = jnp.arange(n_new)[None, :] <= jnp.arange(n_new)[:, None]
    mask = jnp.concatenate([jnp.ones((n_new, past), dtype=bool), new_mask], axis=1)

    def one_sequence(args):
        q_b, ckv_b, kpe_b, pages = args
        ckv = jnp.concatenate([cache_ckv[layer, pages].reshape(past, KV_LORA), ckv_b], axis=0)
        kpe = jnp.concatenate([cache_kpe[layer, pages].reshape(past, QK_ROPE), kpe_b], axis=0)
        k, v = _mla_keys_values(ckv, kpe, w_kv_up, k_norm_g)
        return _attend(q_b[None], k[None], v[None], mask)[0]

    return lax.map(one_sequence, (q, ckv_new, kpe_new, page_table))


def _short_conv(zb, zc, zx, buf, conv_w):
    u = zc * zx
    ext = jnp.concatenate([buf.astype(u.dtype), u], axis=1)
    T = u.shape[1]
    y = conv_w[0] * ext[:, 0:T]
    for j in range(1, CONV_WIDTH):
        y = y + conv_w[j] * ext[:, j:j + T]
    return zb * y, ext[:, -(CONV_WIDTH - 1):]


def _pool_mix(p, buf, pos, pool_w, pool_scale):
    n, T = p.shape[0], p.shape[1]
    ext = jnp.concatenate([buf.astype(p.dtype), p], axis=1)
    extf = ext.astype(jnp.float32)
    csum = jnp.concatenate([jnp.zeros((n, 1, D_POOL), jnp.float32), jnp.cumsum(extf, axis=1)], axis=1)
    end = csum[:, POOL_BUF + 1:POOL_BUF + 1 + T]
    tok = extf[:, POOL_BUF:POOL_BUF + T]
    outs = []
    for g, w in enumerate(POOL_WINDOWS):
        lo, hi = g * POOL_GROUP, (g + 1) * POOL_GROUP
        start = csum[:, POOL_BUF + 1 - w:POOL_BUF + 1 - w + T, lo:hi]
        cnt = jnp.minimum(pos + 1, w).astype(jnp.float32)[None, :, None]
        d = ((end[..., lo:hi] - start) / cnt - tok[..., lo:hi]).astype(p.dtype)
        outs.append(d @ pool_w[g])
    y = jnp.concatenate(outs, axis=-1) * pool_scale
    return y, ext[:, -POOL_BUF:]


def _ssm_mix(u, s_re, s_im, a_re, a_im, b_re, b_im, c_re, c_im, d, log_dt, w_glu):
    n, T = u.shape[0], u.shape[1]
    f32 = jnp.float32
    uf = u.astype(f32).reshape(n, T, N_SSM_GROUPS, SSM_GROUP)
    dt = jnp.exp(log_dt.astype(f32))[:, None]
    ar = jnp.minimum(a_re.astype(f32), -1e-4)
    ai = a_im.astype(f32)
    mag = jnp.exp(dt * ar)
    ab_re, ab_im = mag * jnp.cos(dt * ai), mag * jnp.sin(dt * ai)
    den = ar * ar + ai * ai
    nr, ni = ab_re - 1.0, ab_im
    k_re, k_im = (nr * ar + ni * ai) / den, (ni * ar - nr * ai) / den
    br, bi = b_re.astype(f32), b_im.astype(f32)
    bb_re = k_re[..., None] * br - k_im[..., None] * bi
    bb_im = k_re[..., None] * bi + k_im[..., None] * br
    bu_re = jnp.einsum('btgn,gpn->btgp', uf, bb_re)
    bu_im = jnp.einsum('btgn,gpn->btgp', uf, bb_im)
    sr, si = s_re.astype(f32), s_im.astype(f32)
    bu_re = bu_re.at[:, 0].add(ab_re * sr - ab_im * si)
    bu_im = bu_im.at[:, 0].add(ab_re * si + ab_im * sr)
    a_full_re = jnp.broadcast_to(ab_re, bu_re.shape)
    a_full_im = jnp.broadcast_to(ab_im, bu_im.shape)

    def combine(e1, e2):
        a1r, a1i, b1r, b1i = e1
        a2r, a2i, b2r, b2i = e2
        return (a2r * a1r - a2i * a1i, a2r * a1i + a2i * a1r,
                a2r * b1r - a2i * b1i + b2r, a2r * b1i + a2i * b1r + b2i)

    _, _, x_re, x_im = lax.associative_scan(combine, (a_full_re, a_full_im, bu_re, bu_im), axis=1)
    y = jnp.einsum('gnp,btgp->btgn', c_re.astype(f32), x_re) - jnp.einsum('gnp,btgp->btgn', c_im.astype(f32), x_im)
    y = y.reshape(n, T, D_SSM) + d.astype(f32) * uf.reshape(n, T, D_SSM)
    zg = y.astype(u.dtype) @ w_glu
    out = zg[..., :D_SSM] * jax.nn.sigmoid(zg[..., D_SSM:])
    return out, x_re[:, -1], x_im[:, -1]


def _swiglu(h, wg, wu, wd):
    return (jax.nn.silu(h @ wg) * (h @ wu)) @ wd


def _moe(h, router_w, router_b, wg, wu, wd):
    logits = (h @ router_w).astype(jnp.float32) + router_b.astype(jnp.float32)
    top_v, top_i = lax.top_k(logits, TOP_K)
    top_w = jax.nn.softmax(top_v, axis=-1)
    gates = jnp.einsum('...k,...ke->...e', top_w, jax.nn.one_hot(top_i, N_EXPERTS, dtype=jnp.float32)).astype(h.dtype)
    out = jnp.zeros_like(h)
    for e in range(N_EXPERTS):
        out = out + gates[..., e:e + 1] * _swiglu(h, wg[e], wu[e], wd[e])
    return out


def _trunk(x, c, pos0, conv_bufs, pool_bufs, ssm_res, ssm_ims, attend_fn, W):
    n, T = x.shape[0], x.shape[1]
    pos = pos0 + jnp.arange(T)
    cos, sin = _rope_tables(pos)
    ckvs, kpes, convs, pools, sres, sims = [], [], [], [], [], []
    for l in range(DEPTH):
        mod = jax.nn.silu(c) @ W['ada_w'][l] + W['ada_b'][l]
        sh1, sc1, g1, sh2, sc2, g2 = jnp.split(mod[:, None, :], 6, axis=-1)
        h = _rmsnorm(x, W['norm_mix_g'][l]) * (1 + sc1) + sh1
        z = h @ W['w_in'][l]
        z_q, z_kv, z_kr, z_b, z_c, z_x, z_pool, z_ssm, z_gate = jnp.split(z, _split_points(), axis=-1)
        q = (_rmsnorm(z_q, W['q_a_norm_g'][l]) @ W['w_q_up'][l]).reshape(n, T, MLA_HEADS, QK_HEAD)
        q = jnp.concatenate([q[..., :QK_NOPE], _apply_rope(q[..., QK_NOPE:], cos[:, None, :], sin[:, None, :])], axis=-1)
        q = _rmsnorm(q, W['q_norm_g'][l])
        ckv = _rmsnorm(z_kv, W['kv_a_norm_g'][l])
        kpe = _apply_rope(z_kr, cos, sin)
        o_mla = attend_fn(l, q, ckv, kpe).reshape(n, T, MLA_HEADS * V_HEAD)
        o_conv, conv_new = _short_conv(z_b, z_c, z_x, conv_bufs[l], W['conv_w'][l])
        o_pool, pool_new = _pool_mix(z_pool, pool_bufs[l], pos, W['pool_w'][l], W['pool_scale'][l])
        o_ssm, s_re_new, s_im_new = _ssm_mix(z_ssm, ssm_res[l], ssm_ims[l], W['ssm_a_re'][l], W['ssm_a_im'][l],
                                             W['ssm_b_re'][l], W['ssm_b_im'][l], W['ssm_c_re'][l], W['ssm_c_im'][l],
                                             W['ssm_d'][l], W['ssm_log_dt'][l], W['ssm_w_glu'][l])
        gates = jax.nn.sigmoid(z_gate.astype(jnp.float32)).astype(x.dtype).reshape(n, T, N_BRANCHES, D_MODEL)
        merged = (gates[:, :, 0] * (o_conv @ W['w_br_conv'][l]) + gates[:, :, 1] * (o_mla @ W['w_br_mla'][l])
                  + gates[:, :, 2] * (o_pool @ W['w_br_pool'][l]) + gates[:, :, 3] * (o_ssm @ W['w_br_ssm'][l]))
        x = x + g1 * (merged @ W['w_out'][l])
        h2 = _rmsnorm(x, W['norm_ffn_g'][l]) * (1 + sc2) + sh2
        if l % 2 == 0:
            f = _swiglu(h2, W['ffn_w_gate'][l // 2], W['ffn_w_up'][l // 2], W['ffn_w_down'][l // 2])
        else:
            f = _moe(h2, W['moe_router_w'][l // 2], W['moe_router_b'][l // 2],
                     W['moe_w_gate'][l // 2], W['moe_w_up'][l // 2], W['moe_w_down'][l // 2])
        x = x + g2 * f
        ckvs.append(ckv); kpes.append(kpe); convs.append(conv_new); pools.append(pool_new)
        sres.append(s_re_new); sims.append(s_im_new)
    return x, jnp.stack(ckvs), jnp.stack(kpes), jnp.stack(convs), jnp.stack(pools), jnp.stack(sres), jnp.stack(sims)


def _normal(k, shape, scale):
    return jax.random.normal(k, shape, jnp.float32) * scale


def setup_inputs(seed: int = 0) -> dict:
    key = jax.random.key(seed)
    ks = jax.random.split(key, 48)
    n_pages = PAST_LEN // PAGE_SIZE
    n_used = DEC_BATCH * n_pages
    n_pool = n_used + max(1, n_used // 4)
    page_table = jax.random.permutation(ks[6], n_pool)[:n_used].reshape(DEC_BATCH, n_pages).astype(jnp.int32)
    G, P, N = N_SSM_GROUPS, SSM_STATE, SSM_GROUP
    a_im_base = jnp.pi * jnp.arange(P, dtype=jnp.float32)
    return {
        'x_prompt': _normal(ks[0], (BATCH, SEQ, D_MODEL), 1.0),
        'x_sample': _normal(ks[1], (DEC_BATCH, DEC_SEQ, D_MODEL), 1.0),
        'c_prompt': _normal(ks[2], (BATCH, D_MODEL), 1.0),
        'c_sample': _normal(ks[3], (DEC_BATCH, D_MODEL), 1.0),
        'cache_ckv': _normal(ks[4], (DEPTH, n_pool, PAGE_SIZE, KV_LORA), 1.0),
        'cache_kpe': _normal(ks[5], (DEPTH, n_pool, PAGE_SIZE, QK_ROPE), 1.0),
        'page_table': page_table,
        'state_conv': _normal(ks[7], (DEPTH, DEC_BATCH, CONV_WIDTH - 1, D_CONV), 1.0),
        'state_pool': _normal(ks[8], (DEPTH, DEC_BATCH, POOL_BUF, D_POOL), 1.0),
        'state_ssm_re': _normal(ks[9], (DEPTH, DEC_BATCH, G, P), 0.1),
        'state_ssm_im': _normal(ks[10], (DEPTH, DEC_BATCH, G, P), 0.1),
        'ada_w': _normal(ks[11], (DEPTH, D_MODEL, 6 * D_MODEL), 0.5 * D_MODEL ** -0.5),
        'ada_b': _normal(ks[12], (DEPTH, 6 * D_MODEL), 0.01),
        'norm_mix_g': 1.0 + _normal(ks[13], (DEPTH, D_MODEL), 0.02),
        'norm_ffn_g': 1.0 + _normal(ks[14], (DEPTH, D_MODEL), 0.02),
        'w_in': _normal(ks[15], (DEPTH, D_MODEL, D_IN), D_MODEL ** -0.5),
        'q_a_norm_g': 1.0 + _normal(ks[16], (DEPTH, Q_LORA), 0.02),
        'w_q_up': _normal(ks[17], (DEPTH, Q_LORA, MLA_HEADS * QK_HEAD), Q_LORA ** -0.5),
        'kv_a_norm_g': 1.0 + _normal(ks[18], (DEPTH, KV_LORA), 0.02),
        'w_kv_up': _normal(ks[19], (DEPTH, KV_LORA, MLA_HEADS * (QK_NOPE + V_HEAD)), KV_LORA ** -0.5),
        'q_norm_g': 1.0 + _normal(ks[20], (DEPTH, QK_HEAD), 0.02),
        'k_norm_g': 1.0 + _normal(ks[21], (DEPTH, QK_HEAD), 0.02),
        'conv_w': _normal(ks[22], (DEPTH, CONV_WIDTH, D_CONV), CONV_WIDTH ** -0.5),
        'pool_w': _normal(ks[23], (DEPTH, N_POOL_GROUPS, POOL_GROUP, POOL_GROUP), POOL_GROUP ** -0.5),
        'pool_scale': 1.0 + _normal(ks[24], (DEPTH, D_POOL), 0.1),
        'ssm_a_re': -0.5 + _normal(ks[25], (DEPTH, G, P), 0.01),
        'ssm_a_im': a_im_base + _normal(ks[26], (DEPTH, G, P), 0.01),
        'ssm_b_re': _normal(ks[27], (DEPTH, G, P, N), (2 * N) ** -0.5),
        'ssm_b_im': _normal(ks[28], (DEPTH, G, P, N), (2 * N) ** -0.5),
        'ssm_c_re': _normal(ks[29], (DEPTH, G, N, P), P ** -0.5),
        'ssm_c_im': _normal(ks[30], (DEPTH, G, N, P), P ** -0.5),
        'ssm_d': _normal(ks[31], (DEPTH, D_SSM), 1.0),
        'ssm_log_dt': jax.random.uniform(ks[32], (DEPTH, G), jnp.float32, math.log(DT_MIN), math.log(DT_MAX)),
        'ssm_w_glu': _normal(ks[33], (DEPTH, D_SSM, 2 * D_SSM), D_SSM ** -0.5),
        'w_br_conv': _normal(ks[34], (DEPTH, D_CONV, D_MODEL), D_CONV ** -0.5),
        'w_br_mla': _normal(ks[35], (DEPTH, MLA_HEADS * V_HEAD, D_MODEL), (MLA_HEADS * V_HEAD) ** -0.5),
        'w_br_pool': _normal(ks[36], (DEPTH, D_POOL, D_MODEL), D_POOL ** -0.5),
        'w_br_ssm': _normal(ks[37], (DEPTH, D_SSM, D_MODEL), D_SSM ** -0.5),
        'w_out': _normal(ks[38], (DEPTH, D_MODEL, D_MODEL), D_MODEL ** -0.5),
        'ffn_w_gate': _normal(ks[39], (N_DENSE, D_MODEL, D_FF), D_MODEL ** -0.5),
        'ffn_w_up': _normal(ks[40], (N_DENSE, D_MODEL, D_FF), D_MODEL ** -0.5),
        'ffn_w_down': _normal(ks[41], (N_DENSE, D_FF, D_MODEL), D_FF ** -0.5),
        'moe_router_w': _normal(ks[42], (N_MOE, D_MODEL, N_EXPERTS), D_MODEL ** -0.5),
        'moe_router_b': _normal(ks[43], (N_MOE, N_EXPERTS), 0.01),
        'moe_w_gate': _normal(ks[44], (N_MOE, N_EXPERTS, D_MODEL, D_FF_EXPERT), D_MODEL ** -0.5),
        'moe_w_up': _normal(ks[45], (N_MOE, N_EXPERTS, D_MODEL, D_FF_EXPERT), D_MODEL ** -0.5),
        'moe_w_down': _normal(ks[46], (N_MOE, N_EXPERTS, D_FF_EXPERT, D_MODEL), D_FF_EXPERT ** -0.5),
    }


def reference(x_prompt, x_sample, c_prompt, c_sample, cache_ckv, cache_kpe, page_table, state_conv, state_pool,
              state_ssm_re, state_ssm_im, ada_w, ada_b, norm_mix_g, norm_ffn_g, w_in, q_a_norm_g, w_q_up,
              kv_a_norm_g, w_kv_up, q_norm_g, k_norm_g, conv_w, pool_w, pool_scale, ssm_a_re, ssm_a_im,
              ssm_b_re, ssm_b_im, ssm_c_re, ssm_c_im, ssm_d, ssm_log_dt, ssm_w_glu, w_br_conv, w_br_mla,
              w_br_pool, w_br_ssm, w_out, ffn_w_gate, ffn_w_up, ffn_w_down, moe_router_w, moe_router_b,
              moe_w_gate, moe_w_up, moe_w_down):
    W = dict(ada_w=ada_w, ada_b=ada_b, norm_mix_g=norm_mix_g, norm_ffn_g=norm_ffn_g, w_in=w_in,
             q_a_norm_g=q_a_norm_g, w_q_up=w_q_up, kv_a_norm_g=kv_a_norm_g, w_kv_up=w_kv_up,
             q_norm_g=q_norm_g, k_norm_g=k_norm_g, conv_w=conv_w, pool_w=pool_w, pool_scale=pool_scale,
             ssm_a_re=ssm_a_re, ssm_a_im=ssm_a_im, ssm_b_re=ssm_b_re, ssm_b_im=ssm_b_im,
             ssm_c_re=ssm_c_re, ssm_c_im=ssm_c_im, ssm_d=ssm_d, ssm_log_dt=ssm_log_dt, ssm_w_glu=ssm_w_glu,
             w_br_conv=w_br_conv, w_br_mla=w_br_mla, w_br_pool=w_br_pool, w_br_ssm=w_br_ssm, w_out=w_out,
             ffn_w_gate=ffn_w_gate, ffn_w_up=ffn_w_up, ffn_w_down=ffn_w_down,
             moe_router_w=moe_router_w, moe_router_b=moe_router_b,
             moe_w_gate=moe_w_gate, moe_w_up=moe_w_up, moe_w_down=moe_w_down)

    def prompt_attn(l, q, ckv, kpe):
        k, v = _mla_keys_values(ckv, kpe, w_kv_up[l], k_norm_g[l])
        return _causal_block_attention(q, k, v)

    def sample_attn(l, q, ckv, kpe):
        return _paged_attention(q, ckv, kpe, cache_ckv, cache_kpe, page_table, l, w_kv_up[l], k_norm_g[l])

    nb = x_prompt.shape[0]
    zero_conv = jnp.zeros((DEPTH, nb, CONV_WIDTH - 1, D_CONV), x_prompt.dtype)
    zero_pool = jnp.zeros((DEPTH, nb, POOL_BUF, D_POOL), x_prompt.dtype)
    zero_ssm = jnp.zeros((DEPTH, nb, N_SSM_GROUPS, SSM_STATE), jnp.float32)
    y_prompt, ckv_p, kpe_p, conv_p, pool_p, ssm_re_p, ssm_im_p = _trunk(
        x_prompt, c_prompt, 0, zero_conv, zero_pool, zero_ssm, zero_ssm, prompt_attn, W)

    past_len = page_table.shape[1] * PAGE_SIZE
    y_sample, ckv_s, kpe_s, conv_s, pool_s, ssm_re_s, ssm_im_s = _trunk(
        x_sample, c_sample, past_len, state_conv, state_pool, state_ssm_re, state_ssm_im, sample_attn, W)

    return (y_prompt, y_sample, ckv_p, kpe_p, conv_p, pool_p, ssm_re_p, ssm_im_p,
            ckv_s, kpe_s, conv_s, pool_s, ssm_re_s, ssm_im_s)
```

```python
import functools
import math

import jax
import jax.numpy as jnp
from jax import lax
from jax.experimental import pallas as pl
from jax.experimental.pallas import tpu as pltpu

F32 = jnp.float32
BF16 = jnp.bfloat16

D = 1024
DEPTH = 4
PAGE = 128
HEADS = 8
Q_LORA = 384
KV_LORA = 256
NOPE = 64
ROPE = 32
QK_HEAD = NOPE + ROPE
V_HEAD = 64
HP = 128
ROPE_BASE = 10000.0
NEG_INF = -1e30
DC = 384
CONV_W = 3
POOL_WINDOWS = (2, 4, 8, 16)
POOL_GROUP = DC // 4
POOL_BUF = 15
SSM_GROUP = 16
SSM_G = DC // SSM_GROUP
SSM_P = 64
SSM_W = SSM_G * SSM_P
D_FF = 2816
N_EXP = 8
D_FFE = 1408
EPS = 1e-6
SCALE = QK_HEAD ** -0.5

O_Q = 0
O_KV = O_Q + Q_LORA
O_KR = O_KV + KV_LORA
O_B = O_KR + HP
O_C = O_B + DC
O_X = O_C + DC
O_P = O_X + DC
O_S = O_P + DC
O_G = O_S + DC
DZ = O_G + 4 * D

VMEM_LIMIT = 56 * 1024 * 1024


def _const_spec(shape):
    nd = len(shape)
    return pl.BlockSpec(shape, lambda *_: (0,) * nd, pipeline_mode=pl.Buffered(1))


def _params(sem):
    return pltpu.CompilerParams(dimension_semantics=sem, vmem_limit_bytes=VMEM_LIMIT)


def _dot(a, b):
    return jnp.dot(a, b, preferred_element_type=F32)


def _dot_nt(a, b):
    return lax.dot_general(a, b, (((1,), (1,)), ((), ())), preferred_element_type=F32)


def _silu(x):
    return x * jax.nn.sigmoid(x)


def _ada_kernel(c_ref, w_ref, b_ref, o_ref):
    a = _silu(c_ref[...]).astype(BF16)
    o_ref[...] = _dot(a, w_ref[...].astype(BF16)) + b_ref[...]


def _ada(c_all, ada_w, ada_b):
    n = c_all.shape[0]
    return pl.pallas_call(
        _ada_kernel,
        grid=(DEPTH, 6),
        in_specs=[pl.BlockSpec((n, D), lambda l, j: (0, 0)),
                  pl.BlockSpec((None, D, D), lambda l, j: (l, 0, j)),
                  pl.BlockSpec((None, None, 1, D), lambda l, j: (l, j, 0, 0))],
        out_specs=pl.BlockSpec((None, None, n, D), lambda l, j: (l, j, 0, 0)),
        out_shape=jax.ShapeDtypeStruct((DEPTH, 6, n, D), F32),
        compiler_params=_params(("arbitrary", "arbitrary")),
        name="ada",
    )(c_all, ada_w, ada_b.reshape(DEPTH, 6, 1, D))


def _rope(x, rc, rd, ru):
    return x * rc + pltpu.roll(x, HP - ROPE // 2, 1) * rd + pltpu.roll(x, ROPE // 2, 1) * ru


def _mix_in_kernel(x_ref, sc_ref, sh_ref, ng_ref, win_ref, qag_ref, wq_ref, qg_ref, kvg_ref, wk_ref, wv_ref,
                   kg_ref, rc_ref, rd_ref, ru_ref, cw_ref, icnt_ref, pw_ref, ps_ref, are_ref, aim_ref, bbd_ref,
                   cbd_ref, sd_ref, wglu_ref, conv0_ref, pool0_ref, sre0_ref, sim0_ref,
                   q_ref, k_ref, v_ref, ckv_ref, kpe_ref, gate_ref, oc_ref, op_ref, os_ref,
                   convo_ref, poolo_ref, sreo_ref, simo_ref,
                   cext, pext, bu, st_re, st_im):
    i = pl.program_id(0)
    tc, nb = x_ref.shape[0], x_ref.shape[1]
    rows = tc * nb

    @pl.when(i == 0)
    def _():
        cext[0:CONV_W - 1] = conv0_ref[...]
        pext[0:POOL_BUF] = pool0_ref[...]
        st_re[...] = sre0_ref[...]
        st_im[...] = sim0_ref[...]

    x3 = x_ref[...]
    ms = jnp.mean(x3 * x3, axis=-1, keepdims=True)
    h3 = (x3 * lax.rsqrt(ms + EPS) * ng_ref[...]) * (1.0 + sc_ref[...]) + sh_ref[...]
    h = h3.reshape(rows, D).astype(BF16)

    def seg(lo, width):
        return _dot(h, win_ref[:, lo:lo + width])

    def bcast_t(ref, width):
        return jnp.broadcast_to(ref[...], (tc, nb, width)).reshape(rows, width)

    rc, rd, ru = bcast_t(rc_ref, HP), bcast_t(rd_ref, HP), bcast_t(ru_ref, HP)

    zq = seg(O_Q, Q_LORA)
    qa = (zq * lax.rsqrt(jnp.mean(zq * zq, axis=-1, keepdims=True) + EPS) * qag_ref[...]).astype(BF16)
    qf = _dot(qa, wq_ref[...])
    qgain = qg_ref[...] * SCALE
    for hd in range(HEADS):
        qh = _rope(qf[:, hd * HP:(hd + 1) * HP], rc, rd, ru)
        ss = jnp.sum(qh * qh, axis=-1, keepdims=True)
        qh = qh * lax.rsqrt(ss * (1.0 / QK_HEAD) + EPS) * qgain
        q_ref[:, :, hd * HP:(hd + 1) * HP] = qh.reshape(tc, nb, HP).astype(BF16)

    zkv = seg(O_KV, KV_LORA)
    ckv = zkv * lax.rsqrt(jnp.mean(zkv * zkv, axis=-1, keepdims=True) + EPS) * kvg_ref[...]
    ckv_ref[...] = ckv.reshape(tc, nb, KV_LORA)
    ckv_b = ckv.astype(BF16)
    kr = _rope(seg(O_KR, HP), rc, rd, ru)
    kpe_ref[...] = kr.reshape(tc, nb, HP)
    kf = _dot(ckv_b, wk_ref[...])
    for hd in range(HEADS):
        kh = kf[:, hd * HP:(hd + 1) * HP] + kr
        ss = jnp.sum(kh * kh, axis=-1, keepdims=True)
        kh = kh * lax.rsqrt(ss * (1.0 / QK_HEAD) + EPS) * kg_ref[...]
        k_ref[:, :, hd * HP:(hd + 1) * HP] = kh.reshape(tc, nb, HP).astype(BF16)
    v_ref[...] = _dot(ckv_b, wv_ref[...]).reshape(tc, nb, HEADS * HP).astype(BF16)

    u3 = (seg(O_C, DC) * seg(O_X, DC)).reshape(tc, nb, DC)
    cext[CONV_W - 1:CONV_W - 1 + tc] = u3
    ec = cext[...]
    cw = cw_ref[...]
    y3 = cw[0:1] * ec[0:tc] + cw[1:2] * ec[1:tc + 1] + cw[2:3] * ec[2:tc + 2]
    oc_ref[...] = (seg(O_B, DC).reshape(tc, nb, DC) * y3).astype(BF16)
    tail = ec[tc:tc + CONV_W - 1]
    cext[0:CONV_W - 1] = tail
    convo_ref[...] = tail

    p3 = seg(O_P, DC).reshape(tc, nb, DC)
    pext[POOL_BUF:POOL_BUF + tc] = p3
    e0 = pext[...]
    s2 = e0[1:] + e0[:-1]
    s4 = s2[2:] + s2[:-2]
    s8 = s4[4:] + s4[:-4]
    s16 = s8[8:] + s8[:-8]
    lane = lax.broadcasted_iota(jnp.int32, (tc, nb, DC), 2)
    win = jnp.where(lane < POOL_GROUP, s2[14:14 + tc],
                    jnp.where(lane < 2 * POOL_GROUP, s4[12:12 + tc],
                              jnp.where(lane < 3 * POOL_GROUP, s8[8:8 + tc], s16)))
    dpool = (win * icnt_ref[...] - p3).reshape(rows, DC).astype(BF16)
    op_ref[...] = (_dot(dpool, pw_ref[...]) * ps_ref[...]).reshape(tc, nb, DC).astype(BF16)
    ptail = e0[tc:tc + POOL_BUF]
    pext[0:POOL_BUF] = ptail
    poolo_ref[...] = ptail

    us = seg(O_S, DC)
    bu[...] = _dot(us.astype(BF16), bbd_ref[...])
    if tc == 1:
        sr, si = st_re[...], st_im[...]
        ar, ai = are_ref[...], aim_ref[...]
        nr = ar * sr - ai * si + bu[:, 0:SSM_W]
        ni = ar * si + ai * sr + bu[:, SSM_W:2 * SSM_W]
        bu[:, 0:SSM_W] = nr
        bu[:, SSM_W:2 * SSM_W] = ni
        st_re[...] = nr
        st_im[...] = ni
    else:
        cwid = 512
        for c in range(SSM_W // cwid):
            lo = c * cwid
            ar = jnp.broadcast_to(are_ref[:, lo:lo + cwid], (nb, cwid))
            ai = jnp.broadcast_to(aim_ref[:, lo:lo + cwid], (nb, cwid))

            def step(t, carry, lo=lo, ar=ar, ai=ai):
                sr, si = carry
                r0 = pl.multiple_of(t * nb, nb)
                nr = ar * sr - ai * si + bu[pl.ds(r0, nb), lo:lo + cwid]
                ni = ar * si + ai * sr + bu[pl.ds(r0, nb), SSM_W + lo:SSM_W + lo + cwid]
                bu[pl.ds(r0, nb), lo:lo + cwid] = nr
                bu[pl.ds(r0, nb), SSM_W + lo:SSM_W + lo + cwid] = ni
                return nr, ni

            sr, si = lax.fori_loop(0, tc, step, (st_re[:, lo:lo + cwid], st_im[:, lo:lo + cwid]))
            st_re[:, lo:lo + cwid] = sr
            st_im[:, lo:lo + cwid] = si
    sreo_ref[...] = st_re[...]
    simo_ref[...] = st_im[...]
    ys = _dot(bu[...].astype(BF16), cbd_ref[...]) + sd_ref[...] * us
    zg = _dot(ys.astype(BF16), wglu_ref[...])
    os_ref[...] = (zg[:, 0:DC] * jax.nn.sigmoid(zg[:, DC:2 * DC])).reshape(tc, nb, DC).astype(BF16)

    for c in range(4):
        g = jax.nn.sigmoid(seg(O_G + c * D, D))
        gate_ref[:, :, c * D:(c + 1) * D] = g.reshape(tc, nb, D).astype(BF16)


def _mix_in(x, sc, sh, lw, tabs, states, tc):
    T, nb, _ = x.shape
    rows = tc * nb

    def tspec(width):
        return pl.BlockSpec((tc, nb, width), lambda i: (i, 0, 0))

    def ttab(width):
        return pl.BlockSpec((tc, 1, width), lambda i: (i, 0, 0))

    const_in = [sc, sh, lw["ng1"], lw["w_in"], lw["qag"], lw["wq"], lw["qg"], lw["kvg"], lw["wk"], lw["wv"], lw["kg"]]
    tab_in = [tabs["rc"], tabs["rd"], tabs["ru"]]
    const_mid = [lw["conv_w"]]
    const_tail = [lw["pool_w"], lw["pool_scale"], lw["a_re"], lw["a_im"], lw["bbd"], lw["cbd"], lw["ssm_d"],
                  lw["w_glu"], states[0], states[1], states[2], states[3]]
    in_specs = ([tspec(D)] + [_const_spec(a.shape) for a in const_in] + [ttab(HP)] * 3
                + [_const_spec(a.shape) for a in const_mid] + [ttab(DC)]
                + [_const_spec(a.shape) for a in const_tail])
    out_shape = [jax.ShapeDtypeStruct((T, nb, HEADS * HP), BF16)] * 3 + [
        jax.ShapeDtypeStruct((T, nb, KV_LORA), F32), jax.ShapeDtypeStruct((T, nb, HP), F32),
        jax.ShapeDtypeStruct((T, nb, 4 * D), BF16)] + [jax.ShapeDtypeStruct((T, nb, DC), BF16)] * 3 + [
        jax.ShapeDtypeStruct((CONV_W - 1, nb, DC), F32), jax.ShapeDtypeStruct((POOL_BUF, nb, DC), F32),
        jax.ShapeDtypeStruct((nb, SSM_W), F32), jax.ShapeDtypeStruct((nb, SSM_W), F32)]
    out_specs = ([tspec(HEADS * HP)] * 3 + [tspec(KV_LORA), tspec(HP), tspec(4 * D)] + [tspec(DC)] * 3
                 + [pl.BlockSpec((CONV_W - 1, nb, DC), lambda i: (0, 0, 0)),
                    pl.BlockSpec((POOL_BUF, nb, DC), lambda i: (0, 0, 0)),
                    pl.BlockSpec((nb, SSM_W), lambda i: (0, 0)), pl.BlockSpec((nb, SSM_W), lambda i: (0, 0))])
    scratch = [pltpu.VMEM((CONV_W - 1 + tc, nb, DC), F32), pltpu.VMEM((POOL_BUF + tc, nb, DC), F32),
               pltpu.VMEM((rows, 2 * SSM_W), F32), pltpu.VMEM((nb, SSM_W), F32), pltpu.VMEM((nb, SSM_W), F32)]
    return pl.pallas_call(
        _mix_in_kernel,
        grid=(T // tc,),
        in_specs=in_specs,
        out_specs=out_specs,
        out_shape=out_shape,
        scratch_shapes=scratch,
        compiler_params=_params(("arbitrary",)),
        name="mix_in",
    )(x, *const_in, *tab_in, *const_mid, tabs["icnt"], *const_tail)


def _attn_kernel(q_ref, k_ref, v_ref, o_ref, m_sc, l_sc, acc_sc):
    i = pl.program_id(2)
    j = pl.program_id(3)
    tq, tk = q_ref.shape[0], k_ref.shape[0]

    @pl.when(j == 0)
    def _():
        m_sc[...] = jnp.full_like(m_sc, NEG_INF)
        l_sc[...] = jnp.zeros_like(l_sc)
        acc_sc[...] = jnp.zeros_like(acc_sc)

    @pl.when(j <= i)
    def _():
        s = _dot_nt(q_ref[...], k_ref[...])
        qpos = i * tq + lax.broadcasted_iota(jnp.int32, (tq, tk), 0)
        kpos = j * tk + lax.broadcasted_iota(jnp.int32, (tq, tk), 1)
        s = jnp.where(kpos <= qpos, s, NEG_INF)
        m_prev = m_sc[...]
        m_new = jnp.maximum(m_prev, jnp.max(s, axis=-1, keepdims=True))
        alpha = jnp.exp(m_prev - m_new)
        p = jnp.exp(s - m_new)
        l_sc[...] = alpha * l_sc[...] + jnp.sum(p, axis=-1, keepdims=True)
        acc_sc[...] = alpha * acc_sc[...] + _dot(p.astype(BF16), v_ref[...])
        m_sc[...] = m_new

    @pl.when(j == i)
    def _():
        o_ref[...] = (acc_sc[...] / l_sc[...]).astype(o_ref.dtype)


def _prompt_attention(q, k, v, tq):
    T, B, W = q.shape
    q2, k2, v2 = (a.reshape(T, B * W) for a in (q, k, v))
    nt = T // tq
    qspec = pl.BlockSpec((tq, HP), lambda b, h, i, j: (i, b * HEADS + h))
    kspec = pl.BlockSpec((tq, HP), lambda b, h, i, j: (jnp.minimum(j, i), b * HEADS + h))
    out = pl.pallas_call(
        _attn_kernel,
        grid=(B, HEADS, nt, nt),
        in_specs=[qspec, kspec, kspec],
        out_specs=qspec,
        out_shape=jax.ShapeDtypeStruct((T, B * W), BF16),
        scratch_shapes=[pltpu.VMEM((tq, 1), F32), pltpu.VMEM((tq, 1), F32), pltpu.VMEM((tq, HP), F32)],
        compiler_params=_params(("arbitrary",) * 4),
        name="prompt_attention",
    )(q2, k2, v2)
    return out.reshape(T, B, W)


def _paged_kernel(layer, n_pages, ch, pt_ref, q_ref, kn_ref, vn_ref, kg_ref, wkt_ref, wktp_ref, wvp_ref,
                  cckv_ref, ckpe_ref, o_ref, ckv_buf, kpe_buf, sem, m_sc, l_sc, olat_sc, qabs_sc, qpe_sc):
    g = pl.program_id(0)
    total = pl.num_programs(0)
    nch = n_pages // ch
    c = g % nch
    slot = g % 2
    tk = ch * PAGE

    def copies(step, sl):
        out = []
        for p in range(ch):
            page = pt_ref[step * ch + p]
            out.append(pltpu.make_async_copy(cckv_ref.at[layer, page], ckv_buf.at[sl, pl.ds(p * PAGE, PAGE), :],
                                             sem.at[sl, 0]))
            out.append(pltpu.make_async_copy(ckpe_ref.at[layer, page], kpe_buf.at[sl, pl.ds(p * PAGE, PAGE), :],
                                             sem.at[sl, 1]))
        return out

    @pl.when(g == 0)
    def _():
        for cp in copies(g, slot):
            cp.start()

    @pl.when(g + 1 < total)
    def _():
        for cp in copies(g + 1, 1 - slot):
            cp.start()

    row = lax.broadcasted_iota(jnp.int32, (HEADS, HEADS * HP), 0)
    lane = lax.broadcasted_iota(jnp.int32, (HEADS, HEADS * HP), 1)
    own_head = (lane // HP) == row

    @pl.when(c == 0)
    def _():
        m_sc[...] = jnp.full_like(m_sc, NEG_INF)
        l_sc[...] = jnp.zeros_like(l_sc)
        olat_sc[...] = jnp.zeros_like(olat_sc)
        qg = q_ref[...] * kg_ref[...]
        qbd = jnp.where(own_head & ((lane % HP) < NOPE), jnp.tile(qg, (1, HEADS)), 0.0)
        qabs_sc[...] = _dot(qbd.astype(BF16), wktp_ref[...])
        qpe_sc[...] = qg[:, NOPE:NOPE + ROPE]

    for cp in copies(g, slot):
        cp.wait()

    ck = ckv_buf[slot].astype(BF16)
    kp = kpe_buf[slot]
    kn = _dot_nt(wkt_ref[...], ck)
    ss = jnp.sum((kn * kn).reshape(HEADS, NOPE, tk), axis=1)
    kp2 = _dot_nt(jnp.ones((HEADS, ROPE), BF16), (kp * kp).astype(BF16))
    s = _dot_nt(qabs_sc[...].astype(BF16), ck) + _dot_nt(qpe_sc[...].astype(BF16), kp.astype(BF16))
    s = s * lax.rsqrt((ss + kp2) * (1.0 / QK_HEAD) + EPS)
    m_prev = m_sc[...]
    m_new = jnp.maximum(m_prev, jnp.max(s, axis=-1, keepdims=True))
    alpha = jnp.exp(m_prev - m_new)
    p = jnp.exp(s - m_new)
    l_sc[...] = alpha * l_sc[...] + jnp.sum(p, axis=-1, keepdims=True)
    olat_sc[...] = alpha * olat_sc[...] + _dot(p.astype(BF16), ck)
    m_sc[...] = m_new

    @pl.when(c == nch - 1)
    def _():
        s_new = jnp.sum(q_ref[...] * kn_ref[...], axis=-1, keepdims=True)
        m_prev = m_sc[...]
        m_fin = jnp.maximum(m_prev, s_new)
        alpha = jnp.exp(m_prev - m_fin)
        p_new = jnp.exp(s_new - m_fin)
        inv_l = 1.0 / (alpha * l_sc[...] + p_new)
        ov = _dot((alpha * olat_sc[...]).astype(BF16), wvp_ref[...])
        ov = ov + p_new * jnp.tile(vn_ref[...], (1, HEADS))
        o_ref[...] = jnp.sum(jnp.where(own_head, ov * inv_l, 0.0), axis=0, keepdims=True)


def _paged_attention(layer, q, kn, vn, kg, wkt, wktp, wvp, page_table, cache_ckv, cache_kpe, ch):
    nseq, n_pages = page_table.shape
    nch = n_pages // ch
    tk = ch * PAGE

    def per_seq(shape):
        return pl.BlockSpec((None,) + shape, lambda g, pt: (g // nch,) + (0,) * len(shape))

    def const(a):
        nd = a.ndim
        return pl.BlockSpec(a.shape, lambda g, pt: (0,) * nd, pipeline_mode=pl.Buffered(1))

    grid_spec = pltpu.PrefetchScalarGridSpec(
        num_scalar_prefetch=1,
        grid=(nseq * nch,),
        in_specs=[per_seq((HEADS, HP))] * 3 + [const(kg), const(wkt), const(wktp), const(wvp),
                                               pl.BlockSpec(memory_space=pl.ANY), pl.BlockSpec(memory_space=pl.ANY)],
        out_specs=per_seq((1, HEADS * HP)),
        scratch_shapes=[pltpu.VMEM((2, tk, KV_LORA), F32), pltpu.VMEM((2, tk, ROPE), F32),
                        pltpu.SemaphoreType.DMA((2, 2)),
                        pltpu.VMEM((HEADS, 1), F32), pltpu.VMEM((HEADS, 1), F32), pltpu.VMEM((HEADS, KV_LORA), F32),
                        pltpu.VMEM((HEADS, KV_LORA), F32), pltpu.VMEM((HEADS, ROPE), F32)],
    )
    return pl.pallas_call(
        functools.partial(_paged_kernel, layer, n_pages, ch),
        grid_spec=grid_spec,
        out_shape=jax.ShapeDtypeStruct((nseq, 1, HEADS * HP), F32),
        compiler_params=_params(("arbitrary",)),
        name="paged_attention",
    )(page_table.reshape(-1), q, kn, vn, kg, wkt, wktp, wvp, cache_ckv, cache_kpe)


def _merge_norm(x_ref, gate_ref, oc_ref, om_ref, op_ref, os_ref, g1_ref, sh2_ref, sc2_ref,
                wbc_ref, wbm_ref, wbp_ref, wbs_ref, wout_ref, ng_ref):
    tt, nb = x_ref.shape[0], x_ref.shape[1]
    rows = tt * nb

    def branch(o_ref, w_ref, gi):
        y = _dot(o_ref[...].reshape(rows, o_ref.shape[2]), w_ref[...])
        return gate_ref[:, :, gi * D:(gi + 1) * D].reshape(rows, D).astype(F32) * y

    merged = branch(oc_ref, wbc_ref, 0) + branch(om_ref, wbm_ref, 1) + branch(op_ref, wbp_ref, 2) \
        + branch(os_ref, wbs_ref, 3)
    y = _dot(merged.astype(BF16), wout_ref[...])
    x1 = x_ref[...] + g1_ref[...] * y.reshape(tt, nb, D)
    ms = jnp.mean(x1 * x1, axis=-1, keepdims=True)
    h2 = (x1 * lax.rsqrt(ms + EPS) * ng_ref[...]) * (1.0 + sc2_ref[...]) + sh2_ref[...]
    return x1, h2.reshape(rows, D).astype(BF16)


def _mix_out_dense_kernel(x_ref, gate_ref, oc_ref, om_ref, op_ref, os_ref, g1_ref, sh2_ref, sc2_ref, g2_ref,
                          wbc_ref, wbm_ref, wbp_ref, wbs_ref, wout_ref, ng_ref, wg_ref, wu_ref, wd_ref, o_ref):
    tt, nb = x_ref.shape[0], x_ref.shape[1]
    x1, h2 = _merge_norm(x_ref, gate_ref, oc_ref, om_ref, op_ref, os_ref, g1_ref, sh2_ref, sc2_ref,
                         wbc_ref, wbm_ref, wbp_ref, wbs_ref, wout_ref, ng_ref)
    half = D_FF // 2
    f = None
    for c in range(2):
        a = _silu(_dot(h2, wg_ref[:, c * half:(c + 1) * half])) * _dot(h2, wu_ref[:, c * half:(c + 1) * half])
        part = _dot(a.astype(BF16), wd_ref[c * half:(c + 1) * half, :])
        f = part if f is None else f + part
    o_ref[...] = x1 + g2_ref[...] * f.reshape(tt, nb, D)


def _mix_out_router_kernel(x_ref, gate_ref, oc_ref, om_ref, op_ref, os_ref, g1_ref, sh2_ref, sc2_ref,
                           wbc_ref, wbm_ref, wbp_ref, wbs_ref, wout_ref, ng_ref, rw_ref, rb_ref,
                           x1_ref, h2_ref, gates_ref):
    tt, nb = x_ref.shape[0], x_ref.shape[1]
    rows = tt * nb
    x1, h2 = _merge_norm(x_ref, gate_ref, oc_ref, om_ref, op_ref, os_ref, g1_ref, sh2_ref, sc2_ref,
                         wbc_ref, wbm_ref, wbp_ref, wbs_ref, wout_ref, ng_ref)
    x1_ref[...] = x1
    h2_ref[...] = h2.reshape(tt, nb, D)
    logits = _dot(h2, rw_ref[...]) + rb_ref[...]
    lane = lax.broadcasted_iota(jnp.int32, (rows, HP), 1)
    m1 = jnp.max(logits, axis=-1, keepdims=True)
    i1 = jnp.min(jnp.where(logits == m1, lane, HP), axis=-1, keepdims=True)
    rest = jnp.where(lane == i1, -jnp.inf, logits)
    m2 = jnp.max(rest, axis=-1, keepdims=True)
    i2 = jnp.min(jnp.where(rest == m2, lane, HP), axis=-1, keepdims=True)
    e2 = jnp.exp(m2 - m1)
    w1 = 1.0 / (1.0 + e2)
    gates = jnp.where(lane == i1, w1, 0.0) + jnp.where(lane == i2, e2 * w1, 0.0)
    gates_ref[...] = gates.reshape(tt, nb, HP)


def _moe_kernel(h2_ref, gates_ref, x1_ref, g2_ref, wg_ref, wu_ref, wd_ref, o_ref, acc):
    e = pl.program_id(1)
    tt, nb = h2_ref.shape[0], h2_ref.shape[1]
    rows = tt * nb

    @pl.when(e == 0)
    def _():
        acc[...] = jnp.zeros_like(acc)

    h2 = h2_ref[...].reshape(rows, D)
    a = _silu(_dot(h2, wg_ref[...])) * _dot(h2, wu_ref[...])
    y = _dot(a.astype(BF16), wd_ref[...])
    lane = lax.broadcasted_iota(jnp.int32, (rows, HP), 1)
    ge = jnp.sum(jnp.where(lane == e, gates_ref[...].reshape(rows, HP), 0.0), axis=-1, keepdims=True)
    acc[...] += ge * y

    @pl.when(e == N_EXP - 1)
    def _():
        o_ref[...] = x1_ref[...] + g2_ref[...] * acc[...].reshape(tt, nb, D)


def _mix_out(x, gates, oc, om, op, os_, mod, lw, fw, tt, moe_tt):
    T, nb, _ = x.shape

    def tspec(width):
        return pl.BlockSpec((tt, nb, width), lambda i: (i, 0, 0))

    acts = [x, gates, oc, om, op, os_]
    act_specs = [tspec(a.shape[2]) for a in acts]
    g1, sh2, sc2, g2 = mod
    wts = [lw["w_br_conv"], lw["w_br_mla"], lw["w_br_pool"], lw["w_br_ssm"], lw["w_out"], lw["ng2"]]
    if "wg" in fw:
        consts = [g1, sh2, sc2, g2] + wts + [fw["wg"], fw["wu"], fw["wd"]]
        return pl.pallas_call(
            _mix_out_dense_kernel,
            grid=(T // tt,),
            in_specs=act_specs + [_const_spec(a.shape) for a in consts],
            out_specs=tspec(D),
            out_shape=jax.ShapeDtypeStruct((T, nb, D), F32),
            compiler_params=_params(("arbitrary",)),
            name="mix_out_dense",
        )(*acts, *consts)
    consts = [g1, sh2, sc2] + wts + [fw["rw"], fw["rb"]]
    x1, h2, rg = pl.pallas_call(
        _mix_out_router_kernel,
        grid=(T // tt,),
        in_specs=act_specs + [_const_spec(a.shape) for a in consts],
        out_specs=[tspec(D), tspec(D), tspec(HP)],
        out_shape=[jax.ShapeDtypeStruct((T, nb, D), F32), jax.ShapeDtypeStruct((T, nb, D), BF16),
                   jax.ShapeDtypeStruct((T, nb, HP), F32)],
        compiler_params=_params(("arbitrary",)),
        name="mix_out_router",
    )(*acts, *consts)
    mt = moe_tt

    def mspec(width):
        return pl.BlockSpec((mt, nb, width), lambda i, e: (i, 0, 0))

    def wspec(a):
        return pl.BlockSpec((None,) + a.shape[1:], lambda i, e: (e, 0, 0))

    return pl.pallas_call(
        _moe_kernel,
        grid=(T // mt, N_EXP),
        in_specs=[mspec(D), mspec(HP), mspec(D), pl.BlockSpec(g2.shape, lambda i, e: (0, 0)),
                  wspec(fw["ewg"]), wspec(fw["ewu"]), wspec(fw["ewd"])],
        out_specs=mspec(D),
        out_shape=jax.ShapeDtypeStruct((T, nb, D), F32),
        scratch_shapes=[pltpu.VMEM((mt * nb, D), F32)],
        compiler_params=_params(("arbitrary", "arbitrary")),
        name="moe",
    )(h2, rg, x1, g2, fw["ewg"], fw["ewu"], fw["ewd"])


def _pad_heads(w, width):
    k = w.shape[0]
    return jnp.pad(w.reshape(k, HEADS, width), ((0, 0), (0, 0), (0, HP - width))).reshape(k, HEADS * HP)


def _layer_weights(l, W):
    w_in = W["w_in"][l]
    pts = [0]
    for s in (Q_LORA, KV_LORA, ROPE, DC, DC, DC, DC, DC, 4 * D):
        pts.append(pts[-1] + s)
    zq, zkv, zkr, zb, zc, zx, zp, zs, zg = (w_in[:, pts[i]:pts[i + 1]] for i in range(9))
    zkr = jnp.pad(zkr, ((0, 0), (NOPE, HP - NOPE - ROPE)))
    w_in_p = jnp.concatenate([zq, zkv, zkr, zb, zc, zx, zp, zs, zg], axis=1).astype(BF16)
    kv = W["w_kv_up"][l].reshape(KV_LORA, HEADS, NOPE + V_HEAD)
    wk = kv[:, :, :NOPE].reshape(KV_LORA, HEADS * NOPE)
    wv = kv[:, :, NOPE:].reshape(KV_LORA, HEADS * V_HEAD)
    pad_g = lambda g: jnp.pad(g, (0, HP - QK_HEAD)).reshape(1, HP)
    pw = jnp.einsum("gij,gh->gihj", W["pool_w"][l], jnp.eye(4, dtype=F32)).reshape(DC, DC)
    dt = jnp.exp(W["ssm_log_dt"][l])[:, None]
    ar = jnp.minimum(W["ssm_a_re"][l], -1e-4)
    ai = W["ssm_a_im"][l]
    mag = jnp.exp(dt * ar)
    ab_re, ab_im = mag * jnp.cos(dt * ai), mag * jnp.sin(dt * ai)
    den = ar * ar + ai * ai
    nr, ni = ab_re - 1.0, ab_im
    k_re, k_im = (nr * ar + ni * ai) / den, (ni * ar - nr * ai) / den
    br, bi = W["ssm_b_re"][l], W["ssm_b_im"][l]
    bb_re = k_re[..., None] * br - k_im[..., None] * bi
    bb_im = k_re[..., None] * bi + k_im[..., None] * br
    eye_g = jnp.eye(SSM_G, dtype=F32)
    to_bd = lambda b: jnp.einsum("gpn,gh->gnhp", b, eye_g).reshape(DC, SSM_W)
    bbd = jnp.concatenate([to_bd(bb_re), to_bd(bb_im)], axis=1)
    from_bd = lambda c: jnp.einsum("gnp,gh->gphn", c, eye_g).reshape(SSM_W, DC)
    cbd = jnp.concatenate([from_bd(W["ssm_c_re"][l]), -from_bd(W["ssm_c_im"][l])], axis=0)
    return dict(
        ng1=W["norm_mix_g"][l].reshape(1, D), ng2=W["norm_ffn_g"][l].reshape(1, D), w_in=w_in_p,
        qag=W["q_a_norm_g"][l].reshape(1, Q_LORA), wq=_pad_heads(W["w_q_up"][l], QK_HEAD).astype(BF16),
        qg=pad_g(W["q_norm_g"][l]), kvg=W["kv_a_norm_g"][l].reshape(1, KV_LORA),
        wk=_pad_heads(wk, NOPE).astype(BF16), wv=_pad_heads(wv, V_HEAD).astype(BF16), kg=pad_g(W["k_norm_g"][l]),
        wkt=wk.T.astype(BF16), wktp=_pad_heads(wk, NOPE).T.astype(BF16),
        conv_w=W["conv_w"][l], pool_w=pw.astype(BF16), pool_scale=W["pool_scale"][l].reshape(1, DC),
        a_re=ab_re.reshape(1, SSM_W), a_im=ab_im.reshape(1, SSM_W), bbd=bbd.astype(BF16), cbd=cbd.astype(BF16),
        ssm_d=W["ssm_d"][l].reshape(1, DC), w_glu=W["ssm_w_glu"][l].astype(BF16),
        w_br_conv=W["w_br_conv"][l].astype(BF16), w_br_pool=W["w_br_pool"][l].astype(BF16),
        w_br_ssm=W["w_br_ssm"][l].astype(BF16),
        w_br_mla=jnp.pad(W["w_br_mla"][l].reshape(HEADS, V_HEAD, D), ((0, 0), (0, HP - V_HEAD), (0, 0)))
        .reshape(HEADS * HP, D).astype(BF16),
        w_out=W["w_out"][l].astype(BF16),
    )


def _ffn_weights(l, W):
    if l % 2 == 0:
        return dict(wg=W["ffn_w_gate"][l // 2].astype(BF16), wu=W["ffn_w_up"][l // 2].astype(BF16),
                    wd=W["ffn_w_down"][l // 2].astype(BF16))
    rw = jnp.pad(W["moe_router_w"][l // 2], ((0, 0), (0, HP - N_EXP))).astype(BF16)
    rb = jnp.pad(W["moe_router_b"][l // 2], (0, HP - N_EXP), constant_values=NEG_INF).reshape(1, HP)
    return dict(rw=rw, rb=rb, ewg=W["moe_w_gate"][l // 2].astype(BF16), ewu=W["moe_w_up"][l // 2].astype(BF16),
                ewd=W["moe_w_down"][l // 2].astype(BF16))


def _position_tables(pos):
    inv_freq = jnp.power(ROPE_BASE, -jnp.arange(0, ROPE, 2, dtype=F32) / ROPE)
    ang = pos.astype(F32)[:, None] * inv_freq[None, :]
    cos, sin = jnp.cos(ang), jnp.sin(ang)
    half = ROPE // 2
    T = pos.shape[0]
    one = jnp.ones((T, NOPE), F32)
    zero = lambda n: jnp.zeros((T, n), F32)
    rc = jnp.concatenate([one, cos, cos, jnp.ones((T, HP - QK_HEAD), F32)], axis=1)
    rd = jnp.concatenate([zero(NOPE), -sin, zero(HP - NOPE - half)], axis=1)
    ru = jnp.concatenate([zero(NOPE + half), sin, zero(HP - QK_HEAD)], axis=1)
    cnt = jnp.concatenate([jnp.broadcast_to(jnp.minimum(pos + 1, w).astype(F32)[:, None], (T, POOL_GROUP))
                           for w in POOL_WINDOWS], axis=1)
    return dict(rc=rc[:, None], rd=rd[:, None], ru=ru[:, None], icnt=(1.0 / cnt)[:, None])


def _trunk(x, mod, pos0, states, attend, LW, FW, tc, tt, moe_tt):
    T, nb, _ = x.shape
    tabs = _position_tables(pos0 + jnp.arange(T))
    outs = []
    for l in range(DEPTH):
        sh1, sc1, g1, sh2, sc2, g2 = (mod[l, j] for j in range(6))
        q, k, v, ckv, kpe, gates, oc, op, os_, conv_n, pool_n, sre_n, sim_n = _mix_in(
            x, sc1, sh1, LW[l], tabs, states[l], tc)
        om = attend(l, q, k, v)
        x = _mix_out(x, gates, oc, om, op, os_, (g1, sh2, sc2, g2), LW[l], FW[l], tt, moe_tt)
        outs.append((ckv, kpe[:, :, NOPE:NOPE + ROPE], conv_n, pool_n, sre_n, sim_n))
    return x, outs


def kernel(x_prompt, x_sample, c_prompt, c_sample, cache_ckv, cache_kpe, page_table, state_conv, state_pool, state_ssm_re, state_ssm_im, ada_w, ada_b, norm_mix_g, norm_ffn_g, w_in, q_a_norm_g, w_q_up, kv_a_norm_g, w_kv_up, q_norm_g, k_norm_g, conv_w, pool_w, pool_scale, ssm_a_re, ssm_a_im, ssm_b_re, ssm_b_im, ssm_c_re, ssm_c_im, ssm_d, ssm_log_dt, ssm_w_glu, w_br_conv, w_br_mla, w_br_pool, w_br_ssm, w_out, ffn_w_gate, ffn_w_up, ffn_w_down, moe_router_w, moe_router_b, moe_w_gate, moe_w_up, moe_w_down):
    W = dict(norm_mix_g=norm_mix_g, norm_ffn_g=norm_ffn_g, w_in=w_in, q_a_norm_g=q_a_norm_g, w_q_up=w_q_up,
             kv_a_norm_g=kv_a_norm_g, w_kv_up=w_kv_up, q_norm_g=q_norm_g, k_norm_g=k_norm_g, conv_w=conv_w,
             pool_w=pool_w, pool_scale=pool_scale, ssm_a_re=ssm_a_re, ssm_a_im=ssm_a_im, ssm_b_re=ssm_b_re,
             ssm_b_im=ssm_b_im, ssm_c_re=ssm_c_re, ssm_c_im=ssm_c_im, ssm_d=ssm_d, ssm_log_dt=ssm_log_dt,
             ssm_w_glu=ssm_w_glu, w_br_conv=w_br_conv, w_br_mla=w_br_mla, w_br_pool=w_br_pool, w_br_ssm=w_br_ssm,
             w_out=w_out, ffn_w_gate=ffn_w_gate, ffn_w_up=ffn_w_up, ffn_w_down=ffn_w_down,
             moe_router_w=moe_router_w, moe_router_b=moe_router_b, moe_w_gate=moe_w_gate, moe_w_up=moe_w_up,
             moe_w_down=moe_w_down)
    LW = [_layer_weights(l, W) for l in range(DEPTH)]
    FW = [_ffn_weights(l, W) for l in range(DEPTH)]
    B, T, _ = x_prompt.shape
    nseq = x_sample.shape[0]
    n_pages = page_table.shape[1]
    past = n_pages * PAGE

    mod = _ada(jnp.concatenate([c_prompt, c_sample], axis=0), ada_w, ada_b)
    mod_p, mod_s = mod[:, :, :B], mod[:, :, B:]

    tc = min(16, T)
    tq = min(512, T)
    zero_states = [(jnp.zeros((CONV_W - 1, B, DC), F32), jnp.zeros((POOL_BUF, B, DC), F32),
                    jnp.zeros((B, SSM_W), F32), jnp.zeros((B, SSM_W), F32))] * DEPTH

    def prompt_attend(l, q, k, v):
        return _prompt_attention(q, k, v, tq)

    xp, outs_p = _trunk(jnp.swapaxes(x_prompt, 0, 1), mod_p, 0, zero_states, prompt_attend, LW, FW,
                        tc, min(16, T), min(32, T))
    y_prompt = jnp.swapaxes(xp, 0, 1)

    states_s = [(jnp.swapaxes(state_conv[l], 0, 1), jnp.swapaxes(state_pool[l], 0, 1),
                 state_ssm_re[l].reshape(nseq, SSM_W), state_ssm_im[l].reshape(nseq, SSM_W)) for l in range(DEPTH)]
    ch = math.gcd(8, n_pages)

    def sample_attend(l, q, k, v):
        heads = lambda a: a.reshape(nseq, HEADS, HP).astype(F32)
        o = _paged_attention(l, heads(q), heads(k), heads(v), LW[l]["kg"], LW[l]["wkt"], LW[l]["wktp"],
                             LW[l]["wv"], page_table, cache_ckv, cache_kpe, ch)
        return o.reshape(1, nseq, HEADS * HP).astype(BF16)

    xs, outs_s = _trunk(jnp.swapaxes(x_sample, 0, 1), mod_s, past, states_s, sample_attend, LW, FW, 1, 1, 1)
    y_sample = jnp.swapaxes(xs, 0, 1)

    def gather(outs, n):
        ckv = jnp.stack([jnp.swapaxes(o[0], 0, 1) for o in outs])
        kpe = jnp.stack([jnp.swapaxes(o[1], 0, 1) for o in outs])
        conv = jnp.stack([jnp.swapaxes(o[2], 0, 1) for o in outs])
        pool = jnp.stack([jnp.swapaxes(o[3], 0, 1) for o in outs])
        sre = jnp.stack([o[4].reshape(n, SSM_G, SSM_P) for o in outs])
        sim = jnp.stack([o[5].reshape(n, SSM_G, SSM_P) for o in outs])
        return ckv, kpe, conv, pool, sre, sim

    return (y_prompt, y_sample) + gather(outs_p, B) + gather(outs_s, nseq)
```

```python
import functools
import math

import jax
import jax.numpy as jnp
from jax import lax
from jax.experimental import pallas as pl
from jax.experimental.pallas import tpu as pltpu

F32 = jnp.float32
BF16 = jnp.bfloat16

D = 1024
DEPTH = 4
PAGE = 128
HEADS = 8
Q_LORA = 384
KV_LORA = 256
NOPE = 64
ROPE = 32
QK_HEAD = NOPE + ROPE
V_HEAD = 64
HP = 128
ROPE_BASE = 10000.0
NEG_INF = -1e30
DC = 384
CONV_W = 3
POOL_WINDOWS = (2, 4, 8, 16)
POOL_GROUP = DC // 4
POOL_BUF = 15
SSM_GROUP = 16
SSM_G = DC // SSM_GROUP
SSM_P = 64
SSM_W = SSM_G * SSM_P
D_FF = 2816
N_EXP = 8
D_FFE = 1408
EPS = 1e-6
SCALE = QK_HEAD ** -0.5 * math.log2(math.e)

O_Q = 0
O_KV = O_Q + Q_LORA
O_KR = O_KV + KV_LORA
O_B = O_KR + HP
O_C = O_B + DC
O_X = O_C + DC
O_P = O_X + DC
O_S = O_P + DC
O_G = O_S + DC
DZ = O_G + 4 * D

VMEM_LIMIT = 56 * 1024 * 1024


def _const_spec(shape):
    nd = len(shape)
    return pl.BlockSpec(shape, lambda *_: (0,) * nd, pipeline_mode=pl.Buffered(1))


def _params(sem):
    return pltpu.CompilerParams(dimension_semantics=sem, vmem_limit_bytes=VMEM_LIMIT)


def _dot(a, b):
    return jnp.dot(a, b, preferred_element_type=F32)


def _dot_nt(a, b):
    return lax.dot_general(a, b, (((1,), (1,)), ((), ())), preferred_element_type=F32)


def _silu(x):
    return x * jax.nn.sigmoid(x)


def _ada_kernel(c_ref, w_ref, b_ref, o_ref):
    a = _silu(c_ref[...]).astype(BF16)
    o_ref[...] = _dot(a, w_ref[...].astype(BF16)) + b_ref[...]


def _ada(c_all, ada_w, ada_b):
    n = c_all.shape[0]
    return pl.pallas_call(
        _ada_kernel,
        grid=(DEPTH, 6),
        in_specs=[pl.BlockSpec((n, D), lambda l, j: (0, 0)),
                  pl.BlockSpec((None, D, D), lambda l, j: (l, 0, j)),
                  pl.BlockSpec((None, None, 1, D), lambda l, j: (l, j, 0, 0))],
        out_specs=pl.BlockSpec((None, None, n, D), lambda l, j: (l, j, 0, 0)),
        out_shape=jax.ShapeDtypeStruct((DEPTH, 6, n, D), F32),
        compiler_params=_params(("arbitrary", "arbitrary")),
        name="ada",
    )(c_all, ada_w, ada_b.reshape(DEPTH, 6, 1, D))


def _rope(x, rc, rd, ru):
    return x * rc + pltpu.roll(x, HP - ROPE // 2, 1) * rd + pltpu.roll(x, ROPE // 2, 1) * ru


def _mix_in_kernel(flat, x_ref, sc_ref, sh_ref, ng_ref, win_ref, qag_ref, wq_ref, qg_ref, kvg_ref, wk_ref, wv_ref,
                   kg_ref, rc_ref, rd_ref, ru_ref, cw_ref, icnt_ref, pw_ref, ps_ref, are_ref, aim_ref, bbd_ref,
                   cbd_ref, sd_ref, wglu_ref, conv0_ref, pool0_ref, sre0_ref, sim0_ref,
                   q_ref, k_ref, v_ref, ckv_ref, kpe_ref, gate_ref, oc_ref, op_ref, os_ref,
                   convo_ref, poolo_ref, sreo_ref, simo_ref,
                   cext, pext, bu, st_re, st_im, relay):
    i = pl.program_id(0)
    tc, nb = x_ref.shape[0], x_ref.shape[1]
    rows = tc * nb
    W = HEADS * HP

    def emit_heads(dst_ref, val):
        if flat:
            for j in range(HEADS):
                relay[j] = val[:, j * HP:(j + 1) * HP]
            for b in range(nb):
                for j in range(HEADS):
                    lo = b * W + j * HP
                    dst_ref[:, lo:lo + HP] = relay[j, pl.ds(b, tc, stride=nb), :].astype(BF16)
        else:
            dst_ref[...] = val.reshape(tc, nb, W).astype(BF16)

    @pl.when(i == 0)
    def _():
        cext[0:CONV_W - 1] = conv0_ref[...]
        pext[0:POOL_BUF] = pool0_ref[...]
        st_re[...] = sre0_ref[...]
        st_im[...] = sim0_ref[...]

    x3 = x_ref[...]
    ms = jnp.mean(x3 * x3, axis=-1, keepdims=True)
    h3 = (x3 * lax.rsqrt(ms + EPS) * ng_ref[...]) * (1.0 + sc_ref[...]) + sh_ref[...]
    h = h3.reshape(rows, D).astype(BF16)

    def seg(lo, width):
        return _dot(h, win_ref[:, lo:lo + width])

    def bcast_t(ref, width):
        return jnp.broadcast_to(ref[...], (tc, nb, width)).reshape(rows, width)

    rc, rd, ru = bcast_t(rc_ref, HP), bcast_t(rd_ref, HP), bcast_t(ru_ref, HP)

    zq = seg(O_Q, Q_LORA)
    qa = (zq * lax.rsqrt(jnp.mean(zq * zq, axis=-1, keepdims=True) + EPS) * qag_ref[...]).astype(BF16)
    qf = _dot(qa, wq_ref[...])
    qgain = qg_ref[...] * SCALE
    heads = []
    for hd in range(HEADS):
        qh = _rope(qf[:, hd * HP:(hd + 1) * HP], rc, rd, ru)
        ss = jnp.sum(qh * qh, axis=-1, keepdims=True)
        heads.append(qh * lax.rsqrt(ss * (1.0 / QK_HEAD) + EPS) * qgain)
    emit_heads(q_ref, jnp.concatenate(heads, axis=1))

    zkv = seg(O_KV, KV_LORA)
    ckv = zkv * lax.rsqrt(jnp.mean(zkv * zkv, axis=-1, keepdims=True) + EPS) * kvg_ref[...]
    ckv_ref[...] = ckv.reshape(tc, nb, KV_LORA)
    ckv_b = ckv.astype(BF16)
    kr = _rope(seg(O_KR, HP), rc, rd, ru)
    kpe_ref[...] = kr.reshape(tc, nb, HP)
    kf = _dot(ckv_b, wk_ref[...])
    heads = []
    for hd in range(HEADS):
        kh = kf[:, hd * HP:(hd + 1) * HP] + kr
        ss = jnp.sum(kh * kh, axis=-1, keepdims=True)
        heads.append(kh * lax.rsqrt(ss * (1.0 / QK_HEAD) + EPS) * kg_ref[...])
    emit_heads(k_ref, jnp.concatenate(heads, axis=1))
    emit_heads(v_ref, _dot(ckv_b, wv_ref[...]))

    u3 = (seg(O_C, DC) * seg(O_X, DC)).reshape(tc, nb, DC)
    cext[CONV_W - 1:CONV_W - 1 + tc] = u3
    ec = cext[...]
    cw = cw_ref[...]
    y3 = cw[0:1] * ec[0:tc] + cw[1:2] * ec[1:tc + 1] + cw[2:3] * ec[2:tc + 2]
    oc_ref[...] = (seg(O_B, DC).reshape(tc, nb, DC) * y3).astype(BF16)
    tail = ec[tc:tc + CONV_W - 1]
    cext[0:CONV_W - 1] = tail
    convo_ref[...] = tail

    p3 = seg(O_P, DC).reshape(tc, nb, DC)
    pext[POOL_BUF:POOL_BUF + tc] = p3
    e0 = pext[...]
    s2 = e0[1:] + e0[:-1]
    s4 = s2[2:] + s2[:-2]
    s8 = s4[4:] + s4[:-4]
    s16 = s8[8:] + s8[:-8]
    lane = lax.broadcasted_iota(jnp.int32, (tc, nb, DC), 2)
    win = jnp.where(lane < POOL_GROUP, s2[14:14 + tc],
                    jnp.where(lane < 2 * POOL_GROUP, s4[12:12 + tc],
                              jnp.where(lane < 3 * POOL_GROUP, s8[8:8 + tc], s16)))
    dpool = (win * icnt_ref[...] - p3).reshape(rows, DC).astype(BF16)
    op_ref[...] = (_dot(dpool, pw_ref[...]) * ps_ref[...]).reshape(tc, nb, DC).astype(BF16)
    ptail = e0[tc:tc + POOL_BUF]
    pext[0:POOL_BUF] = ptail
    poolo_ref[...] = ptail

    us = seg(O_S, DC)
    ub = us.astype(BF16)
    ca, sa = 256, 1024
    for off in (0, SSM_W):
        bu[:, off:off + sa] = _dot(ub[:, 0:ca], bbd_ref[0:ca, off:off + sa])
        bu[:, off + sa:off + SSM_W] = _dot(ub[:, ca:DC], bbd_ref[ca:DC, off + sa:off + SSM_W])
    if tc == 1:
        sr, si = st_re[...], st_im[...]
        ar, ai = are_ref[...], aim_ref[...]
        nr = ar * sr - ai * si + bu[:, 0:SSM_W]
        ni = ar * si + ai * sr + bu[:, SSM_W:2 * SSM_W]
        bu[:, 0:SSM_W] = nr
        bu[:, SSM_W:2 * SSM_W] = ni
        st_re[...] = nr
        st_im[...] = ni
    else:
        cwid = 512
        for c in range(SSM_W // cwid):
            lo = c * cwid
            ar = jnp.broadcast_to(are_ref[:, lo:lo + cwid], (nb, cwid))
            ai = jnp.broadcast_to(aim_ref[:, lo:lo + cwid], (nb, cwid))

            def step(t, carry, lo=lo, ar=ar, ai=ai):
                sr, si = carry
                r0 = pl.multiple_of(t * nb, nb)
                nr = ar * sr - ai * si + bu[pl.ds(r0, nb), lo:lo + cwid]
                ni = ar * si + ai * sr + bu[pl.ds(r0, nb), SSM_W + lo:SSM_W + lo + cwid]
                bu[pl.ds(r0, nb), lo:lo + cwid] = nr
                bu[pl.ds(r0, nb), SSM_W + lo:SSM_W + lo + cwid] = ni
                return nr, ni

            sr, si = lax.fori_loop(0, tc, step, (st_re[:, lo:lo + cwid], st_im[:, lo:lo + cwid]))
            st_re[:, lo:lo + cwid] = sr
            st_im[:, lo:lo + cwid] = si
    sreo_ref[...] = st_re[...]
    simo_ref[...] = st_im[...]
    ya = (_dot(bu[:, 0:sa].astype(BF16), cbd_ref[0:sa, 0:ca])
          + _dot(bu[:, SSM_W:SSM_W + sa].astype(BF16), cbd_ref[SSM_W:SSM_W + sa, 0:ca]))
    yb = (_dot(bu[:, sa:SSM_W].astype(BF16), cbd_ref[sa:SSM_W, ca:DC])
          + _dot(bu[:, SSM_W + sa:2 * SSM_W].astype(BF16), cbd_ref[SSM_W + sa:2 * SSM_W, ca:DC]))
    ys = jnp.concatenate([ya, yb], axis=1) + sd_ref[...] * us
    zg = _dot(ys.astype(BF16), wglu_ref[...])
    os_ref[...] = (zg[:, 0:DC] * jax.nn.sigmoid(zg[:, DC:2 * DC])).reshape(tc, nb, DC).astype(BF16)

    for c in range(4):
        g = jax.nn.sigmoid(seg(O_G + c * D, D))
        gate_ref[:, :, c * D:(c + 1) * D] = g.reshape(tc, nb, D).astype(BF16)


def _mix_in(x, sc, sh, lw, tabs, states, tc, flat):
    T, nb, _ = x.shape
    rows = tc * nb
    W = HEADS * HP

    def tspec(width):
        return pl.BlockSpec((tc, nb, width), lambda i: (i, 0, 0))

    def ttab(width):
        return pl.BlockSpec((tc, 1, width), lambda i: (i, 0, 0))

    const_in = [sc, sh, lw["ng1"], lw["w_in"], lw["qag"], lw["wq"], lw["qg"], lw["kvg"], lw["wk"], lw["wv"], lw["kg"]]
    tab_in = [tabs["rc"], tabs["rd"], tabs["ru"]]
    const_mid = [lw["conv_w"]]
    const_tail = [lw["pool_w"], lw["pool_scale"], lw["a_re"], lw["a_im"], lw["bbd"], lw["cbd"], lw["ssm_d"],
                  lw["w_glu"], states[0], states[1], states[2], states[3]]
    in_specs = ([tspec(D)] + [_const_spec(a.shape) for a in const_in] + [ttab(HP)] * 3
                + [_const_spec(a.shape) for a in const_mid] + [ttab(DC)]
                + [_const_spec(a.shape) for a in const_tail])
    if flat:
        head_shape = jax.ShapeDtypeStruct((T, nb * W), BF16)
        head_spec = pl.BlockSpec((tc, nb * W), lambda i: (i, 0))
    else:
        head_shape = jax.ShapeDtypeStruct((T, nb, W), BF16)
        head_spec = tspec(W)
    out_shape = [head_shape] * 3 + [
        jax.ShapeDtypeStruct((T, nb, KV_LORA), F32), jax.ShapeDtypeStruct((T, nb, HP), F32),
        jax.ShapeDtypeStruct((T, nb, 4 * D), BF16)] + [jax.ShapeDtypeStruct((T, nb, DC), BF16)] * 3 + [
        jax.ShapeDtypeStruct((CONV_W - 1, nb, DC), F32), jax.ShapeDtypeStruct((POOL_BUF, nb, DC), F32),
        jax.ShapeDtypeStruct((nb, SSM_W), F32), jax.ShapeDtypeStruct((nb, SSM_W), F32)]
    out_specs = ([head_spec] * 3 + [tspec(KV_LORA), tspec(HP), tspec(4 * D)] + [tspec(DC)] * 3
                 + [pl.BlockSpec((CONV_W - 1, nb, DC), lambda i: (0, 0, 0)),
                    pl.BlockSpec((POOL_BUF, nb, DC), lambda i: (0, 0, 0)),
                    pl.BlockSpec((nb, SSM_W), lambda i: (0, 0)), pl.BlockSpec((nb, SSM_W), lambda i: (0, 0))])
    scratch = [pltpu.VMEM((CONV_W - 1 + tc, nb, DC), F32), pltpu.VMEM((POOL_BUF + tc, nb, DC), F32),
               pltpu.VMEM((rows, 2 * SSM_W), F32), pltpu.VMEM((nb, SSM_W), F32), pltpu.VMEM((nb, SSM_W), F32),
               pltpu.VMEM((HEADS, rows if flat else 8, HP), F32)]
    return pl.pallas_call(
        functools.partial(_mix_in_kernel, flat),
        grid=(T // tc,),
        in_specs=in_specs,
        out_specs=out_specs,
        out_shape=out_shape,
        scratch_shapes=scratch,
        compiler_params=_params(("arbitrary",)),
        name="mix_in",
    )(x, *const_in, *tab_in, *const_mid, tabs["icnt"], *const_tail)


def _attn_kernel(tq, q_ref, k_ref, v_ref, o_ref):
    T = q_ref.shape[0]
    row = lax.broadcasted_iota(jnp.int32, (tq, tq), 0)
    col = lax.broadcasted_iota(jnp.int32, (tq, tq), 1)
    for i in range(T // tq):
        lo = i * tq
        q = q_ref[lo:lo + tq, :]
        s_diag = jnp.where(col <= row, _dot_nt(q, k_ref[lo:lo + tq, :]), NEG_INF)
        m = jnp.max(s_diag, axis=-1, keepdims=True)
        if i > 0:
            s_past = _dot_nt(q, k_ref[0:lo, :])
            m = jnp.maximum(m, jnp.max(s_past, axis=-1, keepdims=True))
        p_diag = jnp.exp2(s_diag - m)
        l = jnp.sum(p_diag, axis=-1, keepdims=True)
        acc = _dot(p_diag.astype(BF16), v_ref[lo:lo + tq, :])
        if i > 0:
            p_past = jnp.exp2(s_past - m)
            l = l + jnp.sum(p_past, axis=-1, keepdims=True)
            acc = acc + _dot(p_past.astype(BF16), v_ref[0:lo, :])
        o_ref[lo:lo + tq, :] = (acc / l).astype(o_ref.dtype)


def _prompt_attention(q, k, v, tq):
    T, BW = q.shape
    spec = pl.BlockSpec((T, HP), lambda g: (0, g))
    return pl.pallas_call(
        functools.partial(_attn_kernel, tq),
        grid=(BW // HP,),
        in_specs=[spec, spec, spec],
        out_specs=spec,
        out_shape=jax.ShapeDtypeStruct((T, BW), BF16),
        compiler_params=_params(("arbitrary",)),
        name="prompt_attention",
    )(q, k, v)


def _paged_kernel(layer, n_pages, ch, nsub, pt_ref, q_ref, kn_ref, vn_ref, kg_ref, wkt_ref, wktp_ref, wvp_ref,
                  cckv_ref, ckpe_ref, o_ref, ckv_buf, kpe_buf, sem, m_sc, l_sc, olat_sc, wk_ext, qpe_sc):
    g = pl.program_id(0)
    total = pl.num_programs(0)
    nch = n_pages // ch
    c = g % nch
    slot = g % 2
    tk = ch * PAGE
    nk = HEADS * NOPE

    def copies(step, sl):
        out = []
        for p in range(ch):
            page = pt_ref[step * ch + p]
            out.append(pltpu.make_async_copy(cckv_ref.at[layer, page], ckv_buf.at[sl, pl.ds(p * PAGE, PAGE), :],
                                             sem.at[sl, 0]))
            out.append(pltpu.make_async_copy(ckpe_ref.at[layer, page], kpe_buf.at[sl, :, pl.ds(p * PAGE, PAGE)],
                                             sem.at[sl, 1]))
        return out

    @pl.when(g == 0)
    def _():
        for cp in copies(g, slot):
            cp.start()

    @pl.when(g + 1 < total)
    def _():
        for cp in copies(g + 1, 1 - slot):
            cp.start()

    row = lax.broadcasted_iota(jnp.int32, (HEADS, HEADS * HP), 0)
    lane = lax.broadcasted_iota(jnp.int32, (HEADS, HEADS * HP), 1)
    own_head = (lane // HP) == row

    @pl.when(c == 0)
    def _():
        m_sc[...] = jnp.full_like(m_sc, NEG_INF)
        l_sc[...] = jnp.zeros_like(l_sc)
        olat_sc[...] = jnp.zeros_like(olat_sc)
        qg = q_ref[...] * kg_ref[...]
        qbd = jnp.where(own_head & ((lane % HP) < NOPE), jnp.tile(qg, (1, HEADS)), 0.0)
        qabs = _dot(qbd.astype(BF16), wktp_ref[...])
        wk_ext[0:nk, :] = wkt_ref[...]
        wk_ext[nk:nk + 2 * HEADS, :] = jnp.concatenate([qabs, jnp.zeros_like(qabs)], axis=0).astype(BF16)
        qpe_sc[...] = qg[:, NOPE:NOPE + ROPE]

    for cp in copies(g, slot):
        cp.wait()

    sub = tk // nsub
    s_parts, ck_parts = [], []
    for hf in range(nsub):
        lo = hf * sub
        ck = ckv_buf[slot, lo:lo + sub, :].astype(BF16)
        kp = kpe_buf[slot, :, lo:lo + sub]
        kn_ext = _dot_nt(wk_ext[...], ck)
        kn = kn_ext[0:nk]
        ss = jnp.sum((kn * kn).reshape(NOPE, HEADS, sub), axis=0)
        kp2 = jnp.sum(kp * kp, axis=0, keepdims=True)
        s_h = kn_ext[nk:nk + HEADS] + _dot(qpe_sc[...].astype(BF16), kp.astype(BF16))
        s_parts.append(s_h * lax.rsqrt((ss + kp2) * (1.0 / QK_HEAD) + EPS))
        ck_parts.append(ck)
    s = jnp.concatenate(s_parts, axis=1)
    m_prev = m_sc[...]
    m_new = jnp.maximum(m_prev, jnp.max(s, axis=-1, keepdims=True))
    alpha = jnp.exp2(m_prev - m_new)
    p = jnp.exp2(s - m_new)
    l_sc[...] = alpha * l_sc[...] + jnp.sum(p, axis=-1, keepdims=True)
    pv = _dot(p[:, 0:sub].astype(BF16), ck_parts[0])
    for hf in range(1, nsub):
        pv = pv + _dot(p[:, hf * sub:(hf + 1) * sub].astype(BF16), ck_parts[hf])
    olat_sc[...] = alpha * olat_sc[...] + pv
    m_sc[...] = m_new

    @pl.when(c == nch - 1)
    def _():
        s_new = jnp.sum(q_ref[...] * kn_ref[...], axis=-1, keepdims=True)
        m_prev = m_sc[...]
        m_fin = jnp.maximum(m_prev, s_new)
        alpha = jnp.exp2(m_prev - m_fin)
        p_new = jnp.exp2(s_new - m_fin)
        inv_l = 1.0 / (alpha * l_sc[...] + p_new)
        ov = _dot((alpha * olat_sc[...]).astype(BF16), wvp_ref[...])
        ov = ov + p_new * jnp.tile(vn_ref[...], (1, HEADS))
        o_ref[...] = jnp.sum(jnp.where(own_head, ov * inv_l, 0.0), axis=0, keepdims=True)


def _paged_attention(layer, q, kn, vn, kg, wkt, wktp, wvp, page_table, cache_ckv, cache_kpe, ch):
    nseq, n_pages = page_table.shape
    nch = n_pages // ch
    tk = ch * PAGE

    def per_seq(shape):
        return pl.BlockSpec((None,) + shape, lambda g, pt: (g // nch,) + (0,) * len(shape))

    def const(a):
        nd = a.ndim
        return pl.BlockSpec(a.shape, lambda g, pt: (0,) * nd, pipeline_mode=pl.Buffered(1))

    grid_spec = pltpu.PrefetchScalarGridSpec(
        num_scalar_prefetch=1,
        grid=(nseq * nch,),
        in_specs=[per_seq((HEADS, HP))] * 3 + [const(kg), const(wkt), const(wktp), const(wvp),
                                               pl.BlockSpec(memory_space=pl.ANY), pl.BlockSpec(memory_space=pl.ANY)],
        out_specs=per_seq((1, HEADS * HP)),
        scratch_shapes=[pltpu.VMEM((2, tk, KV_LORA), F32), pltpu.VMEM((2, ROPE, tk), F32),
                        pltpu.SemaphoreType.DMA((2, 2)),
                        pltpu.VMEM((HEADS, 1), F32), pltpu.VMEM((HEADS, 1), F32), pltpu.VMEM((HEADS, KV_LORA), F32),
                        pltpu.VMEM((HEADS * NOPE + 2 * HEADS, KV_LORA), BF16), pltpu.VMEM((HEADS, ROPE), F32)],
    )
    nsub = 2 if ch % 2 == 0 else 1
    return pl.pallas_call(
        functools.partial(_paged_kernel, layer, n_pages, ch, nsub),
        grid_spec=grid_spec,
        out_shape=jax.ShapeDtypeStruct((nseq, 1, HEADS * HP), F32),
        compiler_params=_params(("arbitrary",)),
        name="paged_attention",
    )(page_table.reshape(-1), q, kn, vn, kg, wkt, wktp, wvp, cache_ckv, cache_kpe)


def _merge_norm(flat, relay, x_ref, gate_ref, oc_ref, om_ref, op_ref, os_ref, g1_ref, sh2_ref, sc2_ref,
                wbc_ref, wbm_ref, wbp_ref, wbs_ref, wout_ref, ng_ref):
    tt, nb = x_ref.shape[0], x_ref.shape[1]
    rows = tt * nb
    W = HEADS * HP

    def gated(o, w_ref, gi):
        return gate_ref[:, :, gi * D:(gi + 1) * D].reshape(rows, D).astype(F32) * _dot(o, w_ref[...])

    def branch(o_ref, w_ref, gi):
        return gated(o_ref[...].reshape(rows, o_ref.shape[2]), w_ref, gi)

    if flat:
        for b in range(nb):
            for j in range(HEADS):
                lo = b * W + j * HP
                relay[j, pl.ds(b, tt, stride=nb), :] = om_ref[:, lo:lo + HP].astype(F32)
        mla = gated(jnp.concatenate([relay[j] for j in range(HEADS)], axis=1).astype(BF16), wbm_ref, 1)
    else:
        mla = branch(om_ref, wbm_ref, 1)
    merged = branch(oc_ref, wbc_ref, 0) + mla + branch(op_ref, wbp_ref, 2) + branch(os_ref, wbs_ref, 3)
    y = _dot(merged.astype(BF16), wout_ref[...])
    x1 = x_ref[...] + g1_ref[...] * y.reshape(tt, nb, D)
    ms = jnp.mean(x1 * x1, axis=-1, keepdims=True)
    h2 = (x1 * lax.rsqrt(ms + EPS) * ng_ref[...]) * (1.0 + sc2_ref[...]) + sh2_ref[...]
    return x1, h2.reshape(rows, D).astype(BF16)


def _mix_out_dense_kernel(flat, x_ref, gate_ref, oc_ref, om_ref, op_ref, os_ref, g1_ref, sh2_ref, sc2_ref, g2_ref,
                          wbc_ref, wbm_ref, wbp_ref, wbs_ref, wout_ref, ng_ref, wg_ref, wu_ref, wd_ref, o_ref,
                          relay):
    tt, nb = x_ref.shape[0], x_ref.shape[1]
    x1, h2 = _merge_norm(flat, relay, x_ref, gate_ref, oc_ref, om_ref, op_ref, os_ref, g1_ref, sh2_ref, sc2_ref,
                         wbc_ref, wbm_ref, wbp_ref, wbs_ref, wout_ref, ng_ref)
    half = D_FF // 2
    f = None
    for c in range(2):
        a = _silu(_dot(h2, wg_ref[:, c * half:(c + 1) * half])) * _dot(h2, wu_ref[:, c * half:(c + 1) * half])
        part = _dot(a.astype(BF16), wd_ref[c * half:(c + 1) * half, :])
        f = part if f is None else f + part
    o_ref[...] = x1 + g2_ref[...] * f.reshape(tt, nb, D)


def _mix_out_router_kernel(flat, x_ref, gate_ref, oc_ref, om_ref, op_ref, os_ref, g1_ref, sh2_ref, sc2_ref,
                           wbc_ref, wbm_ref, wbp_ref, wbs_ref, wout_ref, ng_ref, rw_ref, rb_ref,
                           x1_ref, h2_ref, gates_ref, relay):
    tt, nb = x_ref.shape[0], x_ref.shape[1]
    rows = tt * nb
    x1, h2 = _merge_norm(flat, relay, x_ref, gate_ref, oc_ref, om_ref, op_ref, os_ref, g1_ref, sh2_ref, sc2_ref,
                         wbc_ref, wbm_ref, wbp_ref, wbs_ref, wout_ref, ng_ref)
    x1_ref[...] = x1
    h2_ref[...] = h2.reshape(tt, nb, D)
    logits = _dot(h2, rw_ref[...]) + rb_ref[...]
    lane = lax.broadcasted_iota(jnp.int32, (rows, HP), 1)
    m1 = jnp.max(logits, axis=-1, keepdims=True)
    i1 = jnp.min(jnp.where(logits == m1, lane, HP), axis=-1, keepdims=True)
    rest = jnp.where(lane == i1, -jnp.inf, logits)
    m2 = jnp.max(rest, axis=-1, keepdims=True)
    i2 = jnp.min(jnp.where(rest == m2, lane, HP), axis=-1, keepdims=True)
    e2 = jnp.exp(m2 - m1)
    w1 = 1.0 / (1.0 + e2)
    gates = jnp.where(lane == i1, w1, 0.0) + jnp.where(lane == i2, e2 * w1, 0.0)
    gates_ref[...] = gates.reshape(tt, nb, HP)


def _moe_kernel(h2_ref, gates_ref, x1_ref, g2_ref, wg_ref, wu_ref, wd_ref, o_ref, acc):
    e = pl.program_id(1)
    tt, nb = h2_ref.shape[0], h2_ref.shape[1]
    rows = tt * nb

    @pl.when(e == 0)
    def _():
        acc[...] = jnp.zeros_like(acc)

    h2 = h2_ref[...].reshape(rows, D)
    a = _silu(_dot(h2, wg_ref[...])) * _dot(h2, wu_ref[...])
    y = _dot(a.astype(BF16), wd_ref[...])
    lane = lax.broadcasted_iota(jnp.int32, (rows, HP), 1)
    ge = jnp.sum(jnp.where(lane == e, gates_ref[...].reshape(rows, HP), 0.0), axis=-1, keepdims=True)
    acc[...] += ge * y

    @pl.when(e == N_EXP - 1)
    def _():
        o_ref[...] = x1_ref[...] + g2_ref[...] * acc[...].reshape(tt, nb, D)


def _mix_out(x, gates, oc, om, op, os_, mod, lw, fw, tt, moe_tt):
    T, nb, _ = x.shape

    def tspec(width):
        return pl.BlockSpec((tt, nb, width), lambda i: (i, 0, 0))

    flat = om.ndim == 2
    acts = [x, gates, oc, om, op, os_]
    act_specs = [pl.BlockSpec((tt, a.shape[1]), lambda i: (i, 0)) if a.ndim == 2 else tspec(a.shape[2])
                 for a in acts]
    relay = [pltpu.VMEM((HEADS, tt * nb if flat else 8, HP), F32)]
    g1, sh2, sc2, g2 = mod
    wts = [lw["w_br_conv"], lw["w_br_mla"], lw["w_br_pool"], lw["w_br_ssm"], lw["w_out"], lw["ng2"]]
    if "wg" in fw:
        consts = [g1, sh2, sc2, g2] + wts + [fw["wg"], fw["wu"], fw["wd"]]
        return pl.pallas_call(
            functools.partial(_mix_out_dense_kernel, flat),
            grid=(T // tt,),
            in_specs=act_specs + [_const_spec(a.shape) for a in consts],
            out_specs=tspec(D),
            out_shape=jax.ShapeDtypeStruct((T, nb, D), F32),
            scratch_shapes=relay,
            compiler_params=_params(("arbitrary",)),
            name="mix_out_dense",
        )(*acts, *consts)
    consts = [g1, sh2, sc2] + wts + [fw["rw"], fw["rb"]]
    x1, h2, rg = pl.pallas_call(
        functools.partial(_mix_out_router_kernel, flat),
        grid=(T // tt,),
        in_specs=act_specs + [_const_spec(a.shape) for a in consts],
        out_specs=[tspec(D), tspec(D), tspec(HP)],
        out_shape=[jax.ShapeDtypeStruct((T, nb, D), F32), jax.ShapeDtypeStruct((T, nb, D), BF16),
                   jax.ShapeDtypeStruct((T, nb, HP), F32)],
        scratch_shapes=relay,
        compiler_params=_params(("arbitrary",)),
        name="mix_out_router",
    )(*acts, *consts)
    mt = moe_tt

    def mspec(width):
        return pl.BlockSpec((mt, nb, width), lambda i, e: (i, 0, 0))

    def wspec(a):
        return pl.BlockSpec((None,) + a.shape[1:], lambda i, e: (e, 0, 0))

    return pl.pallas_call(
        _moe_kernel,
        grid=(T // mt, N_EXP),
        in_specs=[mspec(D), mspec(HP), mspec(D), pl.BlockSpec(g2.shape, lambda i, e: (0, 0)),
                  wspec(fw["ewg"]), wspec(fw["ewu"]), wspec(fw["ewd"])],
        out_specs=mspec(D),
        out_shape=jax.ShapeDtypeStruct((T, nb, D), F32),
        scratch_shapes=[pltpu.VMEM((mt * nb, D), F32)],
        compiler_params=_params(("arbitrary", "arbitrary")),
        name="moe",
    )(h2, rg, x1, g2, fw["ewg"], fw["ewu"], fw["ewd"])


def _pad_heads(w, width):
    k = w.shape[0]
    return jnp.pad(w.reshape(k, HEADS, width), ((0, 0), (0, 0), (0, HP - width))).reshape(k, HEADS * HP)


def _layer_weights(l, W):
    w_in = W["w_in"][l]
    pts = [0]
    for s in (Q_LORA, KV_LORA, ROPE, DC, DC, DC, DC, DC, 4 * D):
        pts.append(pts[-1] + s)
    zq, zkv, zkr, zb, zc, zx, zp, zs, zg = (w_in[:, pts[i]:pts[i + 1]] for i in range(9))
    zkr = jnp.pad(zkr, ((0, 0), (NOPE, HP - NOPE - ROPE)))
    w_in_p = jnp.concatenate([zq, zkv, zkr, zb, zc, zx, zp, zs, zg], axis=1).astype(BF16)
    kv = W["w_kv_up"][l].reshape(KV_LORA, HEADS, NOPE + V_HEAD)
    wk = kv[:, :, :NOPE].reshape(KV_LORA, HEADS * NOPE)
    wv = kv[:, :, NOPE:].reshape(KV_LORA, HEADS * V_HEAD)
    pad_g = lambda g: jnp.pad(g, (0, HP - QK_HEAD)).reshape(1, HP)
    pw = jnp.einsum("gij,gh->gihj", W["pool_w"][l], jnp.eye(4, dtype=F32)).reshape(DC, DC)
    dt = jnp.exp(W["ssm_log_dt"][l])[:, None]
    ar = jnp.minimum(W["ssm_a_re"][l], -1e-4)
    ai = W["ssm_a_im"][l]
    mag = jnp.exp(dt * ar)
    ab_re, ab_im = mag * jnp.cos(dt * ai), mag * jnp.sin(dt * ai)
    den = ar * ar + ai * ai
    nr, ni = ab_re - 1.0, ab_im
    k_re, k_im = (nr * ar + ni * ai) / den, (ni * ar - nr * ai) / den
    br, bi = W["ssm_b_re"][l], W["ssm_b_im"][l]
    bb_re = k_re[..., None] * br - k_im[..., None] * bi
    bb_im = k_re[..., None] * bi + k_im[..., None] * br
    eye_g = jnp.eye(SSM_G, dtype=F32)
    to_bd = lambda b: jnp.einsum("gpn,gh->gnhp", b, eye_g).reshape(DC, SSM_W)
    bbd = jnp.concatenate([to_bd(bb_re), to_bd(bb_im)], axis=1)
    from_bd = lambda c: jnp.einsum("gnp,gh->gphn", c, eye_g).reshape(SSM_W, DC)
    cbd = jnp.concatenate([from_bd(W["ssm_c_re"][l]), -from_bd(W["ssm_c_im"][l])], axis=0)
    return dict(
        ng1=W["norm_mix_g"][l].reshape(1, D), ng2=W["norm_ffn_g"][l].reshape(1, D), w_in=w_in_p,
        qag=W["q_a_norm_g"][l].reshape(1, Q_LORA), wq=_pad_heads(W["w_q_up"][l], QK_HEAD).astype(BF16),
        qg=pad_g(W["q_norm_g"][l]), kvg=W["kv_a_norm_g"][l].reshape(1, KV_LORA),
        wk=_pad_heads(wk, NOPE).astype(BF16), wv=_pad_heads(wv, V_HEAD).astype(BF16), kg=pad_g(W["k_norm_g"][l]),
        wkt=kv[:, :, :NOPE].transpose(2, 1, 0).reshape(HEADS * NOPE, KV_LORA).astype(BF16),
        wktp=_pad_heads(wk, NOPE).T.astype(BF16),
        conv_w=W["conv_w"][l], pool_w=pw.astype(BF16), pool_scale=W["pool_scale"][l].reshape(1, DC),
        a_re=ab_re.reshape(1, SSM_W), a_im=ab_im.reshape(1, SSM_W), bbd=bbd.astype(BF16), cbd=cbd.astype(BF16),
        ssm_d=W["ssm_d"][l].reshape(1, DC), w_glu=W["ssm_w_glu"][l].astype(BF16),
        w_br_conv=W["w_br_conv"][l].astype(BF16), w_br_pool=W["w_br_pool"][l].astype(BF16),
        w_br_ssm=W["w_br_ssm"][l].astype(BF16),
        w_br_mla=jnp.pad(W["w_br_mla"][l].reshape(HEADS, V_HEAD, D), ((0, 0), (0, HP - V_HEAD), (0, 0)))
        .reshape(HEADS * HP, D).astype(BF16),
        w_out=W["w_out"][l].astype(BF16),
    )


def _ffn_weights(l, W):
    if l % 2 == 0:
        return dict(wg=W["ffn_w_gate"][l // 2].astype(BF16), wu=W["ffn_w_up"][l // 2].astype(BF16),
                    wd=W["ffn_w_down"][l // 2].astype(BF16))
    rw = jnp.pad(W["moe_router_w"][l // 2], ((0, 0), (0, HP - N_EXP))).astype(BF16)
    rb = jnp.pad(W["moe_router_b"][l // 2], (0, HP - N_EXP), constant_values=NEG_INF).reshape(1, HP)
    return dict(rw=rw, rb=rb, ewg=W["moe_w_gate"][l // 2].astype(BF16), ewu=W["moe_w_up"][l // 2].astype(BF16),
                ewd=W["moe_w_down"][l // 2].astype(BF16))


def _position_tables(pos):
    inv_freq = jnp.power(ROPE_BASE, -jnp.arange(0, ROPE, 2, dtype=F32) / ROPE)
    ang = pos.astype(F32)[:, None] * inv_freq[None, :]
    cos, sin = jnp.cos(ang), jnp.sin(ang)
    half = ROPE // 2
    T = pos.shape[0]
    one = jnp.ones((T, NOPE), F32)
    zero = lambda n: jnp.zeros((T, n), F32)
    rc = jnp.concatenate([one, cos, cos, jnp.ones((T, HP - QK_HEAD), F32)], axis=1)
    rd = jnp.concatenate([zero(NOPE), -sin, zero(HP - NOPE - half)], axis=1)
    ru = jnp.concatenate([zero(NOPE + half), sin, zero(HP - QK_HEAD)], axis=1)
    cnt = jnp.concatenate([jnp.broadcast_to(jnp.minimum(pos + 1, w).astype(F32)[:, None], (T, POOL_GROUP))
                           for w in POOL_WINDOWS], axis=1)
    return dict(rc=rc[:, None], rd=rd[:, None], ru=ru[:, None], icnt=(1.0 / cnt)[:, None])


def _trunk(x, mod, pos0, states, attend, LW, FW, tc, tt, moe_tt, flat):
    T, nb, _ = x.shape
    tabs = _position_tables(pos0 + jnp.arange(T))
    outs = []
    for l in range(DEPTH):
        sh1, sc1, g1, sh2, sc2, g2 = (mod[l, j] for j in range(6))
        q, k, v, ckv, kpe, gates, oc, op, os_, conv_n, pool_n, sre_n, sim_n = _mix_in(
            x, sc1, sh1, LW[l], tabs, states[l], tc, flat)
        om = attend(l, q, k, v)
        x = _mix_out(x, gates, oc, om, op, os_, (g1, sh2, sc2, g2), LW[l], FW[l], tt, moe_tt)
        outs.append((ckv, kpe[:, :, NOPE:NOPE + ROPE], conv_n, pool_n, sre_n, sim_n))
    return x, outs


def kernel(x_prompt, x_sample, c_prompt, c_sample, cache_ckv, cache_kpe, page_table, state_conv, state_pool, state_ssm_re, state_ssm_im, ada_w, ada_b, norm_mix_g, norm_ffn_g, w_in, q_a_norm_g, w_q_up, kv_a_norm_g, w_kv_up, q_norm_g, k_norm_g, conv_w, pool_w, pool_scale, ssm_a_re, ssm_a_im, ssm_b_re, ssm_b_im, ssm_c_re, ssm_c_im, ssm_d, ssm_log_dt, ssm_w_glu, w_br_conv, w_br_mla, w_br_pool, w_br_ssm, w_out, ffn_w_gate, ffn_w_up, ffn_w_down, moe_router_w, moe_router_b, moe_w_gate, moe_w_up, moe_w_down):
    W = dict(norm_mix_g=norm_mix_g, norm_ffn_g=norm_ffn_g, w_in=w_in, q_a_norm_g=q_a_norm_g, w_q_up=w_q_up,
             kv_a_norm_g=kv_a_norm_g, w_kv_up=w_kv_up, q_norm_g=q_norm_g, k_norm_g=k_norm_g, conv_w=conv_w,
             pool_w=pool_w, pool_scale=pool_scale, ssm_a_re=ssm_a_re, ssm_a_im=ssm_a_im, ssm_b_re=ssm_b_re,
             ssm_b_im=ssm_b_im, ssm_c_re=ssm_c_re, ssm_c_im=ssm_c_im, ssm_d=ssm_d, ssm_log_dt=ssm_log_dt,
             ssm_w_glu=ssm_w_glu, w_br_conv=w_br_conv, w_br_mla=w_br_mla, w_br_pool=w_br_pool, w_br_ssm=w_br_ssm,
             w_out=w_out, ffn_w_gate=ffn_w_gate, ffn_w_up=ffn_w_up, ffn_w_down=ffn_w_down,
             moe_router_w=moe_router_w, moe_router_b=moe_router_b, moe_w_gate=moe_w_gate, moe_w_up=moe_w_up,
             moe_w_down=moe_w_down)
    LW = [_layer_weights(l, W) for l in range(DEPTH)]
    FW = [_ffn_weights(l, W) for l in range(DEPTH)]
    B, T, _ = x_prompt.shape
    nseq = x_sample.shape[0]
    n_pages = page_table.shape[1]
    past = n_pages * PAGE

    mod = _ada(jnp.concatenate([c_prompt, c_sample], axis=0), ada_w, ada_b)
    mod_p, mod_s = mod[:, :, :B], mod[:, :, B:]

    tc = min(16, T)
    tq = min(256, T)
    zero_states = [(jnp.zeros((CONV_W - 1, B, DC), F32), jnp.zeros((POOL_BUF, B, DC), F32),
                    jnp.zeros((B, SSM_W), F32), jnp.zeros((B, SSM_W), F32))] * DEPTH

    def prompt_attend(l, q, k, v):
        return _prompt_attention(q, k, v, tq)

    xp, outs_p = _trunk(jnp.swapaxes(x_prompt, 0, 1), mod_p, 0, zero_states, prompt_attend, LW, FW,
                        tc, min(16, T), min(32, T), True)
    y_prompt = jnp.swapaxes(xp, 0, 1)

    states_s = [(jnp.swapaxes(state_conv[l], 0, 1), jnp.swapaxes(state_pool[l], 0, 1),
                 state_ssm_re[l].reshape(nseq, SSM_W), state_ssm_im[l].reshape(nseq, SSM_W)) for l in range(DEPTH)]
    ch = math.gcd(16, n_pages)
    cache_kpe_t = jnp.swapaxes(cache_kpe, 2, 3)

    def sample_attend(l, q, k, v):
        heads = lambda a: a.reshape(nseq, HEADS, HP).astype(F32)
        o = _paged_attention(l, heads(q), heads(k), heads(v), LW[l]["kg"], LW[l]["wkt"], LW[l]["wktp"],
                             LW[l]["wv"], page_table, cache_ckv, cache_kpe_t, ch)
        return o.reshape(1, nseq, HEADS * HP).astype(BF16)

    xs, outs_s = _trunk(jnp.swapaxes(x_sample, 0, 1), mod_s, past, states_s, sample_attend, LW, FW, 1, 1, 1,
                        False)
    y_sample = jnp.swapaxes(xs, 0, 1)

    def gather(outs, n):
        ckv = jnp.stack([jnp.swapaxes(o[0], 0, 1) for o in outs])
        kpe = jnp.stack([jnp.swapaxes(o[1], 0, 1) for o in outs])
        conv = jnp.stack([jnp.swapaxes(o[2], 0, 1) for o in outs])
        pool = jnp.stack([jnp.swapaxes(o[3], 0, 1) for o in outs])
        sre = jnp.stack([o[4].reshape(n, SSM_G, SSM_P) for o in outs])
        sim = jnp.stack([o[5].reshape(n, SSM_G, SSM_P) for o in outs])
        return ckv, kpe, conv, pool, sre, sim

    return (y_prompt, y_sample) + gather(outs_p, B) + gather(outs_s, nseq)
```

```python
import functools
import math

import jax
import jax.numpy as jnp
from jax import lax
from jax.experimental import pallas as pl
from jax.experimental.pallas import tpu as pltpu

F32 = jnp.float32
BF16 = jnp.bfloat16

D = 1024
DEPTH = 4
PAGE = 128
HEADS = 8
Q_LORA = 384
KV_LORA = 256
NOPE = 64
ROPE = 32
QK_HEAD = NOPE + ROPE
V_HEAD = 64
HP = 128
ROPE_BASE = 10000.0
NEG_INF = -1e30
DC = 384
CONV_W = 3
POOL_WINDOWS = (2, 4, 8, 16)
POOL_GROUP = DC // 4
POOL_BUF = 15
SSM_GROUP = 16
SSM_G = DC // SSM_GROUP
SSM_P = 64
SSM_W = SSM_G * SSM_P
D_FF = 2816
N_EXP = 8
D_FFE = 1408
EPS = 1e-6
SCALE = QK_HEAD ** -0.5 * math.log2(math.e)

O_Q = 0
O_KV = O_Q + Q_LORA
O_KR = O_KV + KV_LORA
O_B = O_KR + HP
O_C = O_B + DC
O_X = O_C + DC
O_P = O_X + DC
O_S = O_P + DC
O_G = O_S + DC
DZ = O_G + 4 * D

VMEM_LIMIT = 56 * 1024 * 1024
MOE_SUB_ROWS = 512
MOE_CHUNK_ROWS = 160


def _const_spec(shape):
    nd = len(shape)
    return pl.BlockSpec(shape, lambda *_: (0,) * nd, pipeline_mode=pl.Buffered(1))


def _params(sem):
    return pltpu.CompilerParams(dimension_semantics=sem, vmem_limit_bytes=VMEM_LIMIT)


def _dot(a, b):
    return jnp.dot(a, b, preferred_element_type=F32)


def _dot_nt(a, b):
    return lax.dot_general(a, b, (((1,), (1,)), ((), ())), preferred_element_type=F32)


def _silu(x):
    return x * jax.nn.sigmoid(x)


def _ada_kernel(c_ref, w_ref, b_ref, o_ref):
    a = _silu(c_ref[...]).astype(BF16)
    o_ref[...] = _dot(a, w_ref[...].astype(BF16)) + b_ref[...]


def _ada(c_all, ada_w, ada_b):
    n = c_all.shape[0]
    return pl.pallas_call(
        _ada_kernel,
        grid=(DEPTH, 6),
        in_specs=[pl.BlockSpec((n, D), lambda l, j: (0, 0)),
                  pl.BlockSpec((None, D, D), lambda l, j: (l, 0, j)),
                  pl.BlockSpec((None, None, 1, D), lambda l, j: (l, j, 0, 0))],
        out_specs=pl.BlockSpec((None, None, n, D), lambda l, j: (l, j, 0, 0)),
        out_shape=jax.ShapeDtypeStruct((DEPTH, 6, n, D), F32),
        compiler_params=_params(("arbitrary", "arbitrary")),
        name="ada",
    )(c_all, ada_w, ada_b.reshape(DEPTH, 6, 1, D))


def _row_order_swap(n_outer, n_inner):
    rows = n_outer * n_inner
    r_out = lax.broadcasted_iota(jnp.int32, (rows, rows), 0)
    r_in = lax.broadcasted_iota(jnp.int32, (rows, rows), 1)
    src = (r_out % n_outer) * n_inner + r_out // n_outer
    return jnp.where(r_in == src, 1.0, 0.0).astype(BF16)


def _rope(x, rc, rd, ru):
    return x * rc + pltpu.roll(x, HP - ROPE // 2, 1) * rd + pltpu.roll(x, ROPE // 2, 1) * ru


def _mix_in_kernel(flat, x_ref, sc_ref, sh_ref, ng_ref, win_ref, qag_ref, wq_ref, qg_ref, kvg_ref, wk_ref, wv_ref,
                   kg_ref, rc_ref, rd_ref, ru_ref, cw_ref, icnt_ref, pw_ref, ps_ref, are_ref, aim_ref, bbd_ref,
                   cbd_ref, sd_ref, wglu_ref, conv0_ref, pool0_ref, sre0_ref, sim0_ref,
                   q_ref, k_ref, v_ref, ckv_ref, kpe_ref, gate_ref, oc_ref, op_ref, os_ref,
                   convo_ref, poolo_ref, sreo_ref, simo_ref,
                   cext, pext, bu, st_re, st_im):
    i = pl.program_id(0)
    tc, nb = x_ref.shape[0], x_ref.shape[1]
    rows = tc * nb
    W = HEADS * HP

    assert flat or tc == 1

    def emit_heads(dst_ref, val):
        if flat:
            for b in range(nb):
                dst_ref[:, b * W:(b + 1) * W] = val[b * tc:(b + 1) * tc, :].astype(BF16)
        else:
            dst_ref[...] = val.reshape(tc, nb, W).astype(BF16)

    def emit_rows(dst_ref, val):
        dst_ref[...] = val.reshape(dst_ref.shape)

    @pl.when(i == 0)
    def _():
        cext[0:CONV_W - 1] = conv0_ref[...]
        pext[0:POOL_BUF] = pool0_ref[...]
        st_re[...] = sre0_ref[...]
        st_im[...] = sim0_ref[...]

    x3 = x_ref[...]
    ms = jnp.mean(x3 * x3, axis=-1, keepdims=True)
    h3 = (x3 * lax.rsqrt(ms + EPS) * ng_ref[...]) * (1.0 + sc_ref[...]) + sh_ref[...]
    h = h3.reshape(rows, D).astype(BF16)

    def seg(lo, width):
        return _dot(h, win_ref[:, lo:lo + width])

    if flat:
        h_att = _dot(_row_order_swap(tc, nb), h).astype(BF16)
    else:
        h_att = h

    def seg_att(lo, width):
        return _dot(h_att, win_ref[:, lo:lo + width])

    def rope_table(ref):
        return jnp.tile(ref[...], (nb, 1)) if flat else jnp.broadcast_to(ref[...], (rows, HP))

    rc, rd, ru = rope_table(rc_ref), rope_table(rd_ref), rope_table(ru_ref)

    zq = seg_att(O_Q, Q_LORA)
    qa = (zq * lax.rsqrt(jnp.mean(zq * zq, axis=-1, keepdims=True) + EPS) * qag_ref[...]).astype(BF16)
    qf = _dot(qa, wq_ref[...])
    qgain = qg_ref[...] * SCALE
    heads = []
    for hd in range(HEADS):
        qh = _rope(qf[:, hd * HP:(hd + 1) * HP], rc, rd, ru)
        ss = jnp.sum(qh * qh, axis=-1, keepdims=True)
        heads.append(qh * lax.rsqrt(ss * (1.0 / QK_HEAD) + EPS) * qgain)
    emit_heads(q_ref, jnp.concatenate(heads, axis=1))

    zkv = seg_att(O_KV, KV_LORA)
    ckv = zkv * lax.rsqrt(jnp.mean(zkv * zkv, axis=-1, keepdims=True) + EPS) * kvg_ref[...]
    emit_rows(ckv_ref, ckv)
    ckv_b = ckv.astype(BF16)
    kr = _rope(seg_att(O_KR, HP), rc, rd, ru)
    emit_rows(kpe_ref, kr)
    kf = _dot(ckv_b, wk_ref[...])
    heads = []
    for hd in range(HEADS):
        kh = kf[:, hd * HP:(hd + 1) * HP] + kr
        ss = jnp.sum(kh * kh, axis=-1, keepdims=True)
        heads.append(kh * lax.rsqrt(ss * (1.0 / QK_HEAD) + EPS) * kg_ref[...])
    emit_heads(k_ref, jnp.concatenate(heads, axis=1))
    emit_heads(v_ref, _dot(ckv_b, wv_ref[...]))

    u3 = (seg(O_C, DC) * seg(O_X, DC)).reshape(tc, nb, DC)
    cext[CONV_W - 1:CONV_W - 1 + tc] = u3
    ec = cext[...]
    cw = cw_ref[...]
    y3 = cw[0:1] * ec[0:tc] + cw[1:2] * ec[1:tc + 1] + cw[2:3] * ec[2:tc + 2]
    oc_ref[...] = (seg(O_B, DC).reshape(tc, nb, DC) * y3).astype(BF16)
    tail = ec[tc:tc + CONV_W - 1]
    cext[0:CONV_W - 1] = tail
    convo_ref[...] = tail

    p3 = seg(O_P, DC).reshape(tc, nb, DC)
    pext[POOL_BUF:POOL_BUF + tc] = p3
    e0 = pext[...]
    s2 = e0[1:] + e0[:-1]
    s4 = s2[2:] + s2[:-2]
    s8 = s4[4:] + s4[:-4]
    s16 = s8[8:] + s8[:-8]
    lane = lax.broadcasted_iota(jnp.int32, (tc, nb, DC), 2)
    win = jnp.where(lane < POOL_GROUP, s2[14:14 + tc],
                    jnp.where(lane < 2 * POOL_GROUP, s4[12:12 + tc],
                              jnp.where(lane < 3 * POOL_GROUP, s8[8:8 + tc], s16)))
    dpool = (win * icnt_ref[...] - p3).reshape(rows, DC).astype(BF16)
    op_ref[...] = (_dot(dpool, pw_ref[...]) * ps_ref[...]).reshape(tc, nb, DC).astype(BF16)
    ptail = e0[tc:tc + POOL_BUF]
    pext[0:POOL_BUF] = ptail
    poolo_ref[...] = ptail

    us = seg(O_S, DC)
    ub = us.astype(BF16)
    ca, sa = 256, 1024
    for off in (0, SSM_W):
        bu[:, off:off + sa] = _dot(ub[:, 0:ca], bbd_ref[0:ca, off:off + sa])
        bu[:, off + sa:off + SSM_W] = _dot(ub[:, ca:DC], bbd_ref[ca:DC, off + sa:off + SSM_W])
    if tc == 1:
        sr, si = st_re[...], st_im[...]
        ar, ai = are_ref[...], aim_ref[...]
        nr = ar * sr - ai * si + bu[:, 0:SSM_W]
        ni = ar * si + ai * sr + bu[:, SSM_W:2 * SSM_W]
        bu[:, 0:SSM_W] = nr
        bu[:, SSM_W:2 * SSM_W] = ni
        st_re[...] = nr
        st_im[...] = ni
    else:
        cwid = 512
        for c in range(SSM_W // cwid):
            lo = c * cwid
            ar = jnp.broadcast_to(are_ref[:, lo:lo + cwid], (nb, cwid))
            ai = jnp.broadcast_to(aim_ref[:, lo:lo + cwid], (nb, cwid))

            def step(t, carry, lo=lo, ar=ar, ai=ai):
                sr, si = carry
                r0 = pl.multiple_of(t * nb, nb)
                nr = ar * sr - ai * si + bu[pl.ds(r0, nb), lo:lo + cwid]
                ni = ar * si + ai * sr + bu[pl.ds(r0, nb), SSM_W + lo:SSM_W + lo + cwid]
                bu[pl.ds(r0, nb), lo:lo + cwid] = nr
                bu[pl.ds(r0, nb), SSM_W + lo:SSM_W + lo + cwid] = ni
                return nr, ni

            sr, si = lax.fori_loop(0, tc, step, (st_re[:, lo:lo + cwid], st_im[:, lo:lo + cwid]))
            st_re[:, lo:lo + cwid] = sr
            st_im[:, lo:lo + cwid] = si
    sreo_ref[...] = st_re[...]
    simo_ref[...] = st_im[...]
    ya = (_dot(bu[:, 0:sa].astype(BF16), cbd_ref[0:sa, 0:ca])
          + _dot(bu[:, SSM_W:SSM_W + sa].astype(BF16), cbd_ref[SSM_W:SSM_W + sa, 0:ca]))
    yb = (_dot(bu[:, sa:SSM_W].astype(BF16), cbd_ref[sa:SSM_W, ca:DC])
          + _dot(bu[:, SSM_W + sa:2 * SSM_W].astype(BF16), cbd_ref[SSM_W + sa:2 * SSM_W, ca:DC]))
    ys = jnp.concatenate([ya, yb], axis=1) + sd_ref[...] * us
    zg = _dot(ys.astype(BF16), wglu_ref[...])
    os_ref[...] = (zg[:, 0:DC] * jax.nn.sigmoid(zg[:, DC:2 * DC])).reshape(tc, nb, DC).astype(BF16)

    for c in range(4):
        g = jax.nn.sigmoid(seg(O_G + c * D, D))
        gate_ref[:, :, c * D:(c + 1) * D] = g.reshape(tc, nb, D).astype(BF16)


def _mix_in(x, sc, sh, lw, tabs, states, tc, flat):
    T, nb, _ = x.shape
    rows = tc * nb
    W = HEADS * HP

    def tspec(width):
        return pl.BlockSpec((tc, nb, width), lambda i: (i, 0, 0))

    def ttab(width):
        return pl.BlockSpec((tc, 1, width), lambda i: (i, 0, 0))

    def brow(width):
        if flat:
            return jax.ShapeDtypeStruct((nb, T, width), F32), pl.BlockSpec((nb, tc, width), lambda i: (0, i, 0))
        return jax.ShapeDtypeStruct((T, nb, width), F32), tspec(width)

    const_in = [sc, sh, lw["ng1"], lw["w_in"], lw["qag"], lw["wq"], lw["qg"], lw["kvg"], lw["wk"], lw["wv"], lw["kg"]]
    tab_in = [tabs["rc"], tabs["rd"], tabs["ru"]]
    const_mid = [lw["conv_w"]]
    const_tail = [lw["pool_w"], lw["pool_scale"], lw["a_re"], lw["a_im"], lw["bbd"], lw["cbd"], lw["ssm_d"],
                  lw["w_glu"], states[0], states[1], states[2], states[3]]
    in_specs = ([tspec(D)] + [_const_spec(a.shape) for a in const_in]
                + [pl.BlockSpec((tc, HP), lambda i: (i, 0))] * 3
                + [_const_spec(a.shape) for a in const_mid] + [ttab(DC)]
                + [_const_spec(a.shape) for a in const_tail])
    if flat:
        head_shape = jax.ShapeDtypeStruct((T, nb * W), BF16)
        head_spec = pl.BlockSpec((tc, nb * W), lambda i: (i, 0))
    else:
        head_shape = jax.ShapeDtypeStruct((T, nb, W), BF16)
        head_spec = tspec(W)
    out_shape = [head_shape] * 3 + [
        brow(KV_LORA)[0], brow(HP)[0],
        jax.ShapeDtypeStruct((T, nb, 4 * D), BF16)] + [jax.ShapeDtypeStruct((T, nb, DC), BF16)] * 3 + [
        jax.ShapeDtypeStruct((CONV_W - 1, nb, DC), F32), jax.ShapeDtypeStruct((POOL_BUF, nb, DC), F32),
        jax.ShapeDtypeStruct((nb, SSM_W), F32), jax.ShapeDtypeStruct((nb, SSM_W), F32)]
    out_specs = ([head_spec] * 3 + [brow(KV_LORA)[1], brow(HP)[1], tspec(4 * D)] + [tspec(DC)] * 3
                 + [pl.BlockSpec((CONV_W - 1, nb, DC), lambda i: (0, 0, 0)),
                    pl.BlockSpec((POOL_BUF, nb, DC), lambda i: (0, 0, 0)),
                    pl.BlockSpec((nb, SSM_W), lambda i: (0, 0)), pl.BlockSpec((nb, SSM_W), lambda i: (0, 0))])
    scratch = [pltpu.VMEM((CONV_W - 1 + tc, nb, DC), F32), pltpu.VMEM((POOL_BUF + tc, nb, DC), F32),
               pltpu.VMEM((rows, 2 * SSM_W), F32), pltpu.VMEM((nb, SSM_W), F32), pltpu.VMEM((nb, SSM_W), F32)]
    return pl.pallas_call(
        functools.partial(_mix_in_kernel, flat),
        grid=(T // tc,),
        in_specs=in_specs,
        out_specs=out_specs,
        out_shape=out_shape,
        scratch_shapes=scratch,
        compiler_params=_params(("arbitrary",)),
        name="mix_in",
    )(x, *const_in, *tab_in, *const_mid, tabs["icnt"], *const_tail)


def _attn_kernel(tq, q_ref, k_ref, v_ref, o_ref):
    T = q_ref.shape[0]
    row = lax.broadcasted_iota(jnp.int32, (tq, tq), 0)
    col = lax.broadcasted_iota(jnp.int32, (tq, tq), 1)
    for i in range(T // tq):
        lo = i * tq
        q = q_ref[lo:lo + tq, :]
        s_diag = jnp.where(col <= row, _dot_nt(q, k_ref[lo:lo + tq, :]), NEG_INF)
        m = jnp.max(s_diag, axis=-1, keepdims=True)
        if i > 0:
            s_past = _dot_nt(q, k_ref[0:lo, :])
            m = jnp.maximum(m, jnp.max(s_past, axis=-1, keepdims=True))
        p_diag = jnp.exp2(s_diag - m)
        l = jnp.sum(p_diag, axis=-1, keepdims=True)
        acc = _dot(p_diag.astype(BF16), v_ref[lo:lo + tq, :])
        if i > 0:
            p_past = jnp.exp2(s_past - m)
            l = l + jnp.sum(p_past, axis=-1, keepdims=True)
            acc = acc + _dot(p_past.astype(BF16), v_ref[0:lo, :])
        o_ref[lo:lo + tq, :] = (acc / l).astype(o_ref.dtype)


def _prompt_attention(q, k, v, tq):
    T, BW = q.shape
    spec = pl.BlockSpec((T, HP), lambda g: (0, g))
    return pl.pallas_call(
        functools.partial(_attn_kernel, tq),
        grid=(BW // HP,),
        in_specs=[spec, spec, spec],
        out_specs=spec,
        out_shape=jax.ShapeDtypeStruct((T, BW), BF16),
        compiler_params=_params(("arbitrary",)),
        name="prompt_attention",
    )(q, k, v)


def _paged_kernel(layer, n_pages, ch, nsub, pt_ref, q_ref, kn_ref, vn_ref, kg_ref, wkt_ref, wktp_ref, wvp_ref,
                  cckv_ref, ckpe_ref, o_ref, ckv_buf, kpe_buf, sem, m_sc, l_sc, olat_sc, wk_ext, qpe_sc):
    g = pl.program_id(0)
    total = pl.num_programs(0)
    nch = n_pages // ch
    c = g % nch
    slot = g % 2
    tk = ch * PAGE
    nk = HEADS * NOPE

    def copies(step, sl):
        out = []
        for p in range(ch):
            page = pt_ref[step * ch + p]
            out.append(pltpu.make_async_copy(cckv_ref.at[layer, page], ckv_buf.at[sl, pl.ds(p * PAGE, PAGE), :],
                                             sem.at[sl, 0]))
            out.append(pltpu.make_async_copy(ckpe_ref.at[layer, page], kpe_buf.at[sl, :, pl.ds(p * PAGE, PAGE)],
                                             sem.at[sl, 1]))
        return out

    @pl.when(g == 0)
    def _():
        for cp in copies(g, slot):
            cp.start()

    @pl.when(g + 1 < total)
    def _():
        for cp in copies(g + 1, 1 - slot):
            cp.start()

    row = lax.broadcasted_iota(jnp.int32, (HEADS, HEADS * HP), 0)
    lane = lax.broadcasted_iota(jnp.int32, (HEADS, HEADS * HP), 1)
    own_head = (lane // HP) == row

    @pl.when(c == 0)
    def _():
        m_sc[...] = jnp.full_like(m_sc, NEG_INF)
        l_sc[...] = jnp.zeros_like(l_sc)
        olat_sc[...] = jnp.zeros_like(olat_sc)
        qg = q_ref[...] * kg_ref[...]
        qbd = jnp.where(own_head & ((lane % HP) < NOPE), jnp.tile(qg, (1, HEADS)), 0.0)
        qabs = _dot(qbd.astype(BF16), wktp_ref[...])
        wk_ext[0:nk, :] = wkt_ref[...]
        wk_ext[nk:nk + 2 * HEADS, :] = jnp.concatenate([qabs, jnp.zeros_like(qabs)], axis=0).astype(BF16)
        qpe_sc[...] = qg[:, NOPE:NOPE + ROPE]

    for cp in copies(g, slot):
        cp.wait()

    sub = tk // nsub
    s_parts, ck_parts = [], []
    for hf in range(nsub):
        lo = hf * sub
        ck = ckv_buf[slot, lo:lo + sub, :].astype(BF16)
        kp = kpe_buf[slot, :, lo:lo + sub]
        kn_ext = _dot_nt(wk_ext[...], ck)
        kn = kn_ext[0:nk]
        ss = jnp.sum((kn * kn).reshape(NOPE, HEADS, sub), axis=0)
        kp2 = jnp.sum(kp * kp, axis=0, keepdims=True)
        s_h = kn_ext[nk:nk + HEADS] + _dot(qpe_sc[...].astype(BF16), kp.astype(BF16))
        s_parts.append(s_h * lax.rsqrt((ss + kp2) * (1.0 / QK_HEAD) + EPS))
        ck_parts.append(ck)
    s = jnp.concatenate(s_parts, axis=1)
    m_prev = m_sc[...]
    m_new = jnp.maximum(m_prev, jnp.max(s, axis=-1, keepdims=True))
    alpha = jnp.exp2(m_prev - m_new)
    p = jnp.exp2(s - m_new)
    l_sc[...] = alpha * l_sc[...] + jnp.sum(p, axis=-1, keepdims=True)
    pv = _dot(p[:, 0:sub].astype(BF16), ck_parts[0])
    for hf in range(1, nsub):
        pv = pv + _dot(p[:, hf * sub:(hf + 1) * sub].astype(BF16), ck_parts[hf])
    olat_sc[...] = alpha * olat_sc[...] + pv
    m_sc[...] = m_new

    @pl.when(c == nch - 1)
    def _():
        s_new = jnp.sum(q_ref[...] * kn_ref[...], axis=-1, keepdims=True)
        m_prev = m_sc[...]
        m_fin = jnp.maximum(m_prev, s_new)
        alpha = jnp.exp2(m_prev - m_fin)
        p_new = jnp.exp2(s_new - m_fin)
        inv_l = 1.0 / (alpha * l_sc[...] + p_new)
        ov = _dot((alpha * olat_sc[...]).astype(BF16), wvp_ref[...])
        ov = ov + p_new * jnp.tile(vn_ref[...], (1, HEADS))
        o_ref[...] = jnp.sum(jnp.where(own_head, ov * inv_l, 0.0), axis=0, keepdims=True)


def _paged_attention(layer, q, kn, vn, kg, wkt, wktp, wvp, page_table, cache_ckv, cache_kpe, ch):
    nseq, n_pages = page_table.shape
    nch = n_pages // ch
    tk = ch * PAGE

    def per_seq(shape):
        return pl.BlockSpec((None,) + shape, lambda g, pt: (g // nch,) + (0,) * len(shape))

    def const(a):
        nd = a.ndim
        return pl.BlockSpec(a.shape, lambda g, pt: (0,) * nd, pipeline_mode=pl.Buffered(1))

    grid_spec = pltpu.PrefetchScalarGridSpec(
        num_scalar_prefetch=1,
        grid=(nseq * nch,),
        in_specs=[per_seq((HEADS, HP))] * 3 + [const(kg), const(wkt), const(wktp), const(wvp),
                                               pl.BlockSpec(memory_space=pl.ANY), pl.BlockSpec(memory_space=pl.ANY)],
        out_specs=per_seq((1, HEADS * HP)),
        scratch_shapes=[pltpu.VMEM((2, tk, KV_LORA), F32), pltpu.VMEM((2, ROPE, tk), F32),
                        pltpu.SemaphoreType.DMA((2, 2)),
                        pltpu.VMEM((HEADS, 1), F32), pltpu.VMEM((HEADS, 1), F32), pltpu.VMEM((HEADS, KV_LORA), F32),
                        pltpu.VMEM((HEADS * NOPE + 2 * HEADS, KV_LORA), BF16), pltpu.VMEM((HEADS, ROPE), F32)],
    )
    nsub = 2 if ch % 2 == 0 else 1
    return pl.pallas_call(
        functools.partial(_paged_kernel, layer, n_pages, ch, nsub),
        grid_spec=grid_spec,
        out_shape=jax.ShapeDtypeStruct((nseq, 1, HEADS * HP), F32),
        compiler_params=_params(("arbitrary",)),
        name="paged_attention",
    )(page_table.reshape(-1), q, kn, vn, kg, wkt, wktp, wvp, cache_ckv, cache_kpe)


def _merge_norm(flat, x_ref, gate_ref, oc_ref, om_ref, op_ref, os_ref, g1_ref, sh2_ref, sc2_ref,
                wbc_ref, wbm_ref, wbp_ref, wbs_ref, wout_ref, ng_ref):
    tt, nb = x_ref.shape[0], x_ref.shape[1]
    rows = tt * nb
    W = HEADS * HP

    def gated(o, w_ref, gi):
        return gate_ref[:, :, gi * D:(gi + 1) * D].reshape(rows, D).astype(F32) * _dot(o, w_ref[...])

    def branch(o_ref, w_ref, gi):
        return gated(o_ref[...].reshape(rows, o_ref.shape[2]), w_ref, gi)

    if flat:
        by_batch = jnp.concatenate([om_ref[:, b * W:(b + 1) * W] for b in range(nb)], axis=0)
        mla = gated(_dot(_row_order_swap(nb, tt), by_batch).astype(BF16), wbm_ref, 1)
    else:
        mla = branch(om_ref, wbm_ref, 1)
    merged = branch(oc_ref, wbc_ref, 0) + mla + branch(op_ref, wbp_ref, 2) + branch(os_ref, wbs_ref, 3)
    y = _dot(merged.astype(BF16), wout_ref[...])
    x1 = x_ref[...] + g1_ref[...] * y.reshape(tt, nb, D)
    ms = jnp.mean(x1 * x1, axis=-1, keepdims=True)
    h2 = (x1 * lax.rsqrt(ms + EPS) * ng_ref[...]) * (1.0 + sc2_ref[...]) + sh2_ref[...]
    return x1, h2.reshape(rows, D).astype(BF16)


def _mix_out_dense_kernel(flat, x_ref, gate_ref, oc_ref, om_ref, op_ref, os_ref, g1_ref, sh2_ref, sc2_ref, g2_ref,
                          wbc_ref, wbm_ref, wbp_ref, wbs_ref, wout_ref, ng_ref, wg_ref, wu_ref, wd_ref, o_ref):
    tt, nb = x_ref.shape[0], x_ref.shape[1]
    x1, h2 = _merge_norm(flat, x_ref, gate_ref, oc_ref, om_ref, op_ref, os_ref, g1_ref, sh2_ref, sc2_ref,
                         wbc_ref, wbm_ref, wbp_ref, wbs_ref, wout_ref, ng_ref)
    half = D_FF // 2
    f = None
    for c in range(2):
        a = _silu(_dot(h2, wg_ref[:, c * half:(c + 1) * half])) * _dot(h2, wu_ref[:, c * half:(c + 1) * half])
        part = _dot(a.astype(BF16), wd_ref[c * half:(c + 1) * half, :])
        f = part if f is None else f + part
    o_ref[...] = x1 + g2_ref[...] * f.reshape(tt, nb, D)


def _mix_out_router_kernel(flat, x_ref, gate_ref, oc_ref, om_ref, op_ref, os_ref, g1_ref, sh2_ref, sc2_ref,
                           wbc_ref, wbm_ref, wbp_ref, wbs_ref, wout_ref, ng_ref, rw_ref, rb_ref,
                           x1_ref, h2_ref, gates_ref):
    tt, nb = x_ref.shape[0], x_ref.shape[1]
    rows = tt * nb
    x1, h2 = _merge_norm(flat, x_ref, gate_ref, oc_ref, om_ref, op_ref, os_ref, g1_ref, sh2_ref, sc2_ref,
                         wbc_ref, wbm_ref, wbp_ref, wbs_ref, wout_ref, ng_ref)
    x1_ref[...] = x1
    h2_ref[...] = h2.reshape(tt, nb, D)
    logits = _dot(h2, rw_ref[...]) + rb_ref[...]
    lane = lax.broadcasted_iota(jnp.int32, (rows, HP), 1)
    m1 = jnp.max(logits, axis=-1, keepdims=True)
    i1 = jnp.min(jnp.where(logits == m1, lane, HP), axis=-1, keepdims=True)
    rest = jnp.where(lane == i1, -jnp.inf, logits)
    m2 = jnp.max(rest, axis=-1, keepdims=True)
    i2 = jnp.min(jnp.where(rest == m2, lane, HP), axis=-1, keepdims=True)
    e2 = jnp.exp(m2 - m1)
    w1 = 1.0 / (1.0 + e2)
    gates = jnp.where(lane == i1, w1, 0.0) + jnp.where(lane == i2, e2 * w1, 0.0)
    gates_ref[...] = gates.reshape(tt, nb, HP)


def _moe_kernel(rs, cc, h2_ref, gates_ref, x1_ref, g2_ref, wg_ref, wu_ref, wd_ref, o_ref, acc, pos_col, pos_row):
    e = pl.program_id(1)
    tt, nb = h2_ref.shape[0], h2_ref.shape[1]
    rows = tt * nb
    nsb = rows // rs

    @pl.when(e == 0)
    def _():
        acc[...] = jnp.zeros_like(acc)
        r_i = lax.broadcasted_iota(jnp.int32, (rs, rs), 0)
        c_i = lax.broadcasted_iota(jnp.int32, (rs, rs), 1)
        before = jnp.where(c_i < r_i, 1.0, 0.0).astype(BF16)
        for s in range(nsb):
            routed = gates_ref[...].reshape(rows, HP)[s * rs:(s + 1) * rs] > 0.0
            rank = _dot(before, jnp.where(routed, 1.0, 0.0).astype(BF16))
            pc = jnp.where(routed, rank, -1.0)
            pos_col[s] = pc
            pos_row[s] = pc.T

    lane = lax.broadcasted_iota(jnp.int32, (rs, HP), 1)
    for s in range(nsb):
        h2 = h2_ref[...].reshape(rows, D)[s * rs:(s + 1) * rs]
        gates = gates_ref[...].reshape(rows, HP)[s * rs:(s + 1) * rs]
        ge = jnp.sum(jnp.where(lane == e, gates, 0.0), axis=-1, keepdims=True)
        pcol = jnp.sum(jnp.where(lane == e, pos_col[s], 0.0), axis=-1, keepdims=True)
        prow = pos_row[s, pl.ds(e, 1), :]
        count = jnp.sum((prow >= 0.0).astype(jnp.int32))

        def chunk(k, carry, s=s, h2=h2, ge=ge, pcol=pcol, prow=prow):
            base = (k * cc).astype(F32)
            slot_r = lax.broadcasted_iota(jnp.int32, (cc, rs), 0).astype(F32) + base
            pick = jnp.where(slot_r == prow, 1.0, 0.0).astype(BF16)
            xg = _dot(pick, h2).astype(BF16)
            a = _silu(_dot(xg, wg_ref[...])) * _dot(xg, wu_ref[...])
            y = _dot(a.astype(BF16), wd_ref[...]).astype(BF16)
            slot_c = lax.broadcasted_iota(jnp.int32, (rs, cc), 1).astype(F32) + base
            put = jnp.where(slot_c == pcol, 1.0, 0.0).astype(BF16)
            acc[s * rs:(s + 1) * rs, :] += ge * _dot(put, y)
            return carry

        lax.fori_loop(0, (count + cc - 1) // cc, chunk, 0)

    @pl.when(e == N_EXP - 1)
    def _():
        o_ref[...] = x1_ref[...] + g2_ref[...] * acc[...].reshape(tt, nb, D)


def _mix_out(x, gates, oc, om, op, os_, mod, lw, fw, tt, moe_tt):
    T, nb, _ = x.shape

    def tspec(width):
        return pl.BlockSpec((tt, nb, width), lambda i: (i, 0, 0))

    flat = om.ndim == 2
    acts = [x, gates, oc, om, op, os_]
    act_specs = [pl.BlockSpec((tt, a.shape[1]), lambda i: (i, 0)) if a.ndim == 2 else tspec(a.shape[2])
                 for a in acts]
    g1, sh2, sc2, g2 = mod
    wts = [lw["w_br_conv"], lw["w_br_mla"], lw["w_br_pool"], lw["w_br_ssm"], lw["w_out"], lw["ng2"]]
    if "wg" in fw:
        consts = [g1, sh2, sc2, g2] + wts + [fw["wg"], fw["wu"], fw["wd"]]
        return pl.pallas_call(
            functools.partial(_mix_out_dense_kernel, flat),
            grid=(T // tt,),
            in_specs=act_specs + [_const_spec(a.shape) for a in consts],
            out_specs=tspec(D),
            out_shape=jax.ShapeDtypeStruct((T, nb, D), F32),
            compiler_params=_params(("arbitrary",)),
            name="mix_out_dense",
        )(*acts, *consts)
    consts = [g1, sh2, sc2] + wts + [fw["rw"], fw["rb"]]
    x1, h2, rg = pl.pallas_call(
        functools.partial(_mix_out_router_kernel, flat),
        grid=(T // tt,),
        in_specs=act_specs + [_const_spec(a.shape) for a in consts],
        out_specs=[tspec(D), tspec(D), tspec(HP)],
        out_shape=[jax.ShapeDtypeStruct((T, nb, D), F32), jax.ShapeDtypeStruct((T, nb, D), BF16),
                   jax.ShapeDtypeStruct((T, nb, HP), F32)],
        compiler_params=_params(("arbitrary",)),
        name="mix_out_router",
    )(*acts, *consts)
    return _moe(h2, rg, x1, g2, fw, moe_tt)


def _moe(h2, rg, x1, g2, fw, mt):
    T, nb, _ = h2.shape

    def mspec(width, **kw):
        return pl.BlockSpec((mt, nb, width), lambda i, e: (i, 0, 0), **kw)

    once = dict(pipeline_mode=pl.Buffered(1))

    def wspec(a):
        return pl.BlockSpec((None,) + a.shape[1:], lambda i, e: (e, 0, 0))

    rows = mt * nb
    rs = min(MOE_SUB_ROWS, rows)
    cc = min(MOE_CHUNK_ROWS, rs)
    return pl.pallas_call(
        functools.partial(_moe_kernel, rs, cc),
        grid=(T // mt, N_EXP),
        in_specs=[mspec(D, **once), mspec(HP, **once), mspec(D, **once), pl.BlockSpec(g2.shape, lambda i, e: (0, 0)),
                  wspec(fw["ewg"]), wspec(fw["ewu"]), wspec(fw["ewd"])],
        out_specs=mspec(D),
        out_shape=jax.ShapeDtypeStruct((T, nb, D), F32),
        scratch_shapes=[pltpu.VMEM((rows, D), F32), pltpu.VMEM((rows // rs, rs, HP), F32),
                        pltpu.VMEM((rows // rs, HP, rs), F32)],
        compiler_params=_params(("arbitrary", "arbitrary")),
        name="moe",
    )(h2, rg, x1, g2, fw["ewg"], fw["ewu"], fw["ewd"])


def _pad_heads(w, width):
    k = w.shape[0]
    return jnp.pad(w.reshape(k, HEADS, width), ((0, 0), (0, 0), (0, HP - width))).reshape(k, HEADS * HP)


def _layer_weights(l, W):
    w_in = W["w_in"][l]
    pts = [0]
    for s in (Q_LORA, KV_LORA, ROPE, DC, DC, DC, DC, DC, 4 * D):
        pts.append(pts[-1] + s)
    zq, zkv, zkr, zb, zc, zx, zp, zs, zg = (w_in[:, pts[i]:pts[i + 1]] for i in range(9))
    zkr = jnp.pad(zkr, ((0, 0), (NOPE, HP - NOPE - ROPE)))
    w_in_p = jnp.concatenate([zq, zkv, zkr, zb, zc, zx, zp, zs, zg], axis=1).astype(BF16)
    kv = W["w_kv_up"][l].reshape(KV_LORA, HEADS, NOPE + V_HEAD)
    wk = kv[:, :, :NOPE].reshape(KV_LORA, HEADS * NOPE)
    wv = kv[:, :, NOPE:].reshape(KV_LORA, HEADS * V_HEAD)
    pad_g = lambda g: jnp.pad(g, (0, HP - QK_HEAD)).reshape(1, HP)
    pw = jnp.einsum("gij,gh->gihj", W["pool_w"][l], jnp.eye(4, dtype=F32)).reshape(DC, DC)
    dt = jnp.exp(W["ssm_log_dt"][l])[:, None]
    ar = jnp.minimum(W["ssm_a_re"][l], -1e-4)
    ai = W["ssm_a_im"][l]
    mag = jnp.exp(dt * ar)
    ab_re, ab_im = mag * jnp.cos(dt * ai), mag * jnp.sin(dt * ai)
    den = ar * ar + ai * ai
    nr, ni = ab_re - 1.0, ab_im
    k_re, k_im = (nr * ar + ni * ai) / den, (ni * ar - nr * ai) / den
    br, bi = W["ssm_b_re"][l], W["ssm_b_im"][l]
    bb_re = k_re[..., None] * br - k_im[..., None] * bi
    bb_im = k_re[..., None] * bi + k_im[..., None] * br
    eye_g = jnp.eye(SSM_G, dtype=F32)
    to_bd = lambda b: jnp.einsum("gpn,gh->gnhp", b, eye_g).reshape(DC, SSM_W)
    bbd = jnp.concatenate([to_bd(bb_re), to_bd(bb_im)], axis=1)
    from_bd = lambda c: jnp.einsum("gnp,gh->gphn", c, eye_g).reshape(SSM_W, DC)
    cbd = jnp.concatenate([from_bd(W["ssm_c_re"][l]), -from_bd(W["ssm_c_im"][l])], axis=0)
    return dict(
        ng1=W["norm_mix_g"][l].reshape(1, D), ng2=W["norm_ffn_g"][l].reshape(1, D), w_in=w_in_p,
        qag=W["q_a_norm_g"][l].reshape(1, Q_LORA), wq=_pad_heads(W["w_q_up"][l], QK_HEAD).astype(BF16),
        qg=pad_g(W["q_norm_g"][l]), kvg=W["kv_a_norm_g"][l].reshape(1, KV_LORA),
        wk=_pad_heads(wk, NOPE).astype(BF16), wv=_pad_heads(wv, V_HEAD).astype(BF16), kg=pad_g(W["k_norm_g"][l]),
        wkt=kv[:, :, :NOPE].transpose(2, 1, 0).reshape(HEADS * NOPE, KV_LORA).astype(BF16),
        wktp=_pad_heads(wk, NOPE).T.astype(BF16),
        conv_w=W["conv_w"][l], pool_w=pw.astype(BF16), pool_scale=W["pool_scale"][l].reshape(1, DC),
        a_re=ab_re.reshape(1, SSM_W), a_im=ab_im.reshape(1, SSM_W), bbd=bbd.astype(BF16), cbd=cbd.astype(BF16),
        ssm_d=W["ssm_d"][l].reshape(1, DC), w_glu=W["ssm_w_glu"][l].astype(BF16),
        w_br_conv=W["w_br_conv"][l].astype(BF16), w_br_pool=W["w_br_pool"][l].astype(BF16),
        w_br_ssm=W["w_br_ssm"][l].astype(BF16),
        w_br_mla=jnp.pad(W["w_br_mla"][l].reshape(HEADS, V_HEAD, D), ((0, 0), (0, HP - V_HEAD), (0, 0)))
        .reshape(HEADS * HP, D).astype(BF16),
        w_out=W["w_out"][l].astype(BF16),
    )


def _ffn_weights(l, W):
    if l % 2 == 0:
        return dict(wg=W["ffn_w_gate"][l // 2].astype(BF16), wu=W["ffn_w_up"][l // 2].astype(BF16),
                    wd=W["ffn_w_down"][l // 2].astype(BF16))
    rw = jnp.pad(W["moe_router_w"][l // 2], ((0, 0), (0, HP - N_EXP))).astype(BF16)
    rb = jnp.pad(W["moe_router_b"][l // 2], (0, HP - N_EXP), constant_values=NEG_INF).reshape(1, HP)
    return dict(rw=rw, rb=rb, ewg=W["moe_w_gate"][l // 2].astype(BF16), ewu=W["moe_w_up"][l // 2].astype(BF16),
                ewd=W["moe_w_down"][l // 2].astype(BF16))


def _position_tables(pos):
    inv_freq = jnp.power(ROPE_BASE, -jnp.arange(0, ROPE, 2, dtype=F32) / ROPE)
    ang = pos.astype(F32)[:, None] * inv_freq[None, :]
    cos, sin = jnp.cos(ang), jnp.sin(ang)
    half = ROPE // 2
    T = pos.shape[0]
    one = jnp.ones((T, NOPE), F32)
    zero = lambda n: jnp.zeros((T, n), F32)
    rc = jnp.concatenate([one, cos, cos, jnp.ones((T, HP - QK_HEAD), F32)], axis=1)
    rd = jnp.concatenate([zero(NOPE), -sin, zero(HP - NOPE - half)], axis=1)
    ru = jnp.concatenate([zero(NOPE + half), sin, zero(HP - QK_HEAD)], axis=1)
    cnt = jnp.concatenate([jnp.broadcast_to(jnp.minimum(pos + 1, w).astype(F32)[:, None], (T, POOL_GROUP))
                           for w in POOL_WINDOWS], axis=1)
    return dict(rc=rc, rd=rd, ru=ru, icnt=(1.0 / cnt)[:, None])


def _trunk(x, mod, pos0, states, attend, LW, FW, tc, tt, moe_tt, flat):
    T, nb, _ = x.shape
    tabs = _position_tables(pos0 + jnp.arange(T))
    outs = []
    for l in range(DEPTH):
        sh1, sc1, g1, sh2, sc2, g2 = (mod[l, j] for j in range(6))
        q, k, v, ckv, kpe, gates, oc, op, os_, conv_n, pool_n, sre_n, sim_n = _mix_in(
            x, sc1, sh1, LW[l], tabs, states[l], tc, flat)
        om = attend(l, q, k, v)
        x = _mix_out(x, gates, oc, om, op, os_, (g1, sh2, sc2, g2), LW[l], FW[l], tt, moe_tt)
        if not flat:
            ckv, kpe = jnp.swapaxes(ckv, 0, 1), jnp.swapaxes(kpe, 0, 1)
        outs.append((ckv, kpe[:, :, NOPE:NOPE + ROPE], conv_n, pool_n, sre_n, sim_n))
    return x, outs


def kernel(x_prompt, x_sample, c_prompt, c_sample, cache_ckv, cache_kpe, page_table, state_conv, state_pool, state_ssm_re, state_ssm_im, ada_w, ada_b, norm_mix_g, norm_ffn_g, w_in, q_a_norm_g, w_q_up, kv_a_norm_g, w_kv_up, q_norm_g, k_norm_g, conv_w, pool_w, pool_scale, ssm_a_re, ssm_a_im, ssm_b_re, ssm_b_im, ssm_c_re, ssm_c_im, ssm_d, ssm_log_dt, ssm_w_glu, w_br_conv, w_br_mla, w_br_pool, w_br_ssm, w_out, ffn_w_gate, ffn_w_up, ffn_w_down, moe_router_w, moe_router_b, moe_w_gate, moe_w_up, moe_w_down):
    W = dict(norm_mix_g=norm_mix_g, norm_ffn_g=norm_ffn_g, w_in=w_in, q_a_norm_g=q_a_norm_g, w_q_up=w_q_up,
             kv_a_norm_g=kv_a_norm_g, w_kv_up=w_kv_up, q_norm_g=q_norm_g, k_norm_g=k_norm_g, conv_w=conv_w,
             pool_w=pool_w, pool_scale=pool_scale, ssm_a_re=ssm_a_re, ssm_a_im=ssm_a_im, ssm_b_re=ssm_b_re,
             ssm_b_im=ssm_b_im, ssm_c_re=ssm_c_re, ssm_c_im=ssm_c_im, ssm_d=ssm_d, ssm_log_dt=ssm_log_dt,
             ssm_w_glu=ssm_w_glu, w_br_conv=w_br_conv, w_br_mla=w_br_mla, w_br_pool=w_br_pool, w_br_ssm=w_br_ssm,
             w_out=w_out, ffn_w_gate=ffn_w_gate, ffn_w_up=ffn_w_up, ffn_w_down=ffn_w_down,
             moe_router_w=moe_router_w, moe_router_b=moe_router_b, moe_w_gate=moe_w_gate, moe_w_up=moe_w_up,
             moe_w_down=moe_w_down)
    LW = [_layer_weights(l, W) for l in range(DEPTH)]
    FW = [_ffn_weights(l, W) for l in range(DEPTH)]
    B, T, _ = x_prompt.shape
    nseq = x_sample.shape[0]
    n_pages = page_table.shape[1]
    past = n_pages * PAGE

    mod = _ada(jnp.concatenate([c_prompt, c_sample], axis=0), ada_w, ada_b)
    mod_p, mod_s = mod[:, :, :B], mod[:, :, B:]

    tc = min(16, T)
    tq = min(256, T)
    zero_states = [(jnp.zeros((CONV_W - 1, B, DC), F32), jnp.zeros((POOL_BUF, B, DC), F32),
                    jnp.zeros((B, SSM_W), F32), jnp.zeros((B, SSM_W), F32))] * DEPTH

    def prompt_attend(l, q, k, v):
        return _prompt_attention(q, k, v, tq)

    xp, outs_p = _trunk(jnp.swapaxes(x_prompt, 0, 1), mod_p, 0, zero_states, prompt_attend, LW, FW,
                        tc, min(16, T), min(64, T), True)
    y_prompt = jnp.swapaxes(xp, 0, 1)

    states_s = [(jnp.swapaxes(state_conv[l], 0, 1), jnp.swapaxes(state_pool[l], 0, 1),
                 state_ssm_re[l].reshape(nseq, SSM_W), state_ssm_im[l].reshape(nseq, SSM_W)) for l in range(DEPTH)]
    ch = math.gcd(16, n_pages)
    cache_kpe_t = jnp.swapaxes(cache_kpe, 2, 3)

    def sample_attend(l, q, k, v):
        heads = lambda a: a.reshape(nseq, HEADS, HP).astype(F32)
        o = _paged_attention(l, heads(q), heads(k), heads(v), LW[l]["kg"], LW[l]["wkt"], LW[l]["wktp"],
                             LW[l]["wv"], page_table, cache_ckv, cache_kpe_t, ch)
        return o.reshape(1, nseq, HEADS * HP).astype(BF16)

    xs, outs_s = _trunk(jnp.swapaxes(x_sample, 0, 1), mod_s, past, states_s, sample_attend, LW, FW, 1, 1, 1,
                        False)
    y_sample = jnp.swapaxes(xs, 0, 1)

    def gather(outs, n):
        ckv = jnp.stack([o[0] for o in outs])
        kpe = jnp.stack([o[1] for o in outs])
        conv = jnp.stack([jnp.swapaxes(o[2], 0, 1) for o in outs])
        pool = jnp.stack([jnp.swapaxes(o[3], 0, 1) for o in outs])
        sre = jnp.stack([o[4].reshape(n, SSM_G, SSM_P) for o in outs])
        sim = jnp.stack([o[5].reshape(n, SSM_G, SSM_P) for o in outs])
        return ckv, kpe, conv, pool, sre, sim

    return (y_prompt, y_sample) + gather(outs_p, B) + gather(outs_s, nseq)
```

```python
import functools
import math

import jax
import jax.numpy as jnp
from jax import lax
from jax.experimental import pallas as pl
from jax.experimental.pallas import tpu as pltpu

F32 = jnp.float32
BF16 = jnp.bfloat16

D = 1024
DEPTH = 4
PAGE = 128
HEADS = 8
Q_LORA = 384
KV_LORA = 256
NOPE = 64
ROPE = 32
QK_HEAD = NOPE + ROPE
V_HEAD = 64
HP = 128
ROPE_BASE = 10000.0
NEG_INF = -1e30
DC = 384
CONV_W = 3
POOL_WINDOWS = (2, 4, 8, 16)
POOL_GROUP = DC // 4
POOL_BUF = 15
SSM_GROUP = 16
SSM_G = DC // SSM_GROUP
SSM_P = 64
SSM_W = SSM_G * SSM_P
D_FF = 2816
N_EXP = 8
D_FFE = 1408
EPS = 1e-6
SCALE = QK_HEAD ** -0.5 * math.log2(math.e)

O_Q = 0
O_KV = O_Q + Q_LORA
O_KR = O_KV + KV_LORA
O_B = O_KR + HP
O_C = O_B + DC
O_X = O_C + DC
O_P = O_X + DC
O_S = O_P + DC
O_G = O_S + DC
DZ = O_G + 4 * D

VMEM_LIMIT = 56 * 1024 * 1024
MOE_SUB_ROWS = 512
MOE_CHUNK_ROWS = 144


def _const_spec(shape):
    nd = len(shape)
    return pl.BlockSpec(shape, lambda *_: (0,) * nd, pipeline_mode=pl.Buffered(1))


def _params(sem):
    return pltpu.CompilerParams(dimension_semantics=sem, vmem_limit_bytes=VMEM_LIMIT)


def _dot(a, b):
    return jnp.dot(a, b, preferred_element_type=F32)


def _dot_nt(a, b):
    return lax.dot_general(a, b, (((1,), (1,)), ((), ())), preferred_element_type=F32)


def _silu(x):
    return x * jax.nn.sigmoid(x)


def _ada_kernel(c_ref, w_ref, b_ref, o_ref):
    a = _silu(c_ref[...]).astype(BF16)
    o_ref[...] = _dot(a, w_ref[...].astype(BF16)) + b_ref[...]


def _ada(c_all, ada_w, ada_b):
    n = c_all.shape[0]
    return pl.pallas_call(
        _ada_kernel,
        grid=(DEPTH, 6),
        in_specs=[pl.BlockSpec((n, D), lambda l, j: (0, 0)),
                  pl.BlockSpec((None, D, D), lambda l, j: (l, 0, j)),
                  pl.BlockSpec((None, None, 1, D), lambda l, j: (l, j, 0, 0))],
        out_specs=pl.BlockSpec((None, None, n, D), lambda l, j: (l, j, 0, 0)),
        out_shape=jax.ShapeDtypeStruct((DEPTH, 6, n, D), F32),
        compiler_params=_params(("arbitrary", "arbitrary")),
        name="ada",
    )(c_all, ada_w, ada_b.reshape(DEPTH, 6, 1, D))


def _row_order_swap(n_outer, n_inner):
    rows = n_outer * n_inner
    r_out = lax.broadcasted_iota(jnp.int32, (rows, rows), 0)
    r_in = lax.broadcasted_iota(jnp.int32, (rows, rows), 1)
    src = (r_out % n_outer) * n_inner + r_out // n_outer
    return jnp.where(r_in == src, 1.0, 0.0).astype(BF16)


def _rope(x, rc, rd, ru):
    return x * rc + pltpu.roll(x, HP - ROPE // 2, 1) * rd + pltpu.roll(x, ROPE // 2, 1) * ru


def _mix_in_kernel(flat, x_ref, sc_ref, sh_ref, ng_ref, win_ref, qag_ref, wq_ref, qg_ref, kvg_ref, wk_ref, wv_ref,
                   kg_ref, rc_ref, rd_ref, ru_ref, cw_ref, icnt_ref, pw_ref, ps_ref, are_ref, aim_ref, bbd_ref,
                   cbd_ref, sd_ref, wglu_ref, conv0_ref, pool0_ref, sre0_ref, sim0_ref,
                   q_ref, k_ref, v_ref, ckv_ref, kpe_ref, gate_ref, oc_ref, op_ref, os_ref,
                   convo_ref, poolo_ref, sreo_ref, simo_ref,
                   cext, pext, bu, st_re, st_im):
    i = pl.program_id(0)
    tc, nb = x_ref.shape[0], x_ref.shape[1]
    rows = tc * nb
    W = HEADS * HP

    assert flat or tc == 1

    def emit_heads(dst_ref, val):
        if flat:
            for b in range(nb):
                dst_ref[:, b * W:(b + 1) * W] = val[b * tc:(b + 1) * tc, :].astype(BF16)
        else:
            dst_ref[...] = val.reshape(tc, nb, W).astype(BF16)

    def emit_rows(dst_ref, val):
        dst_ref[...] = val.reshape(dst_ref.shape)

    @pl.when(i == 0)
    def _():
        cext[0:CONV_W - 1] = conv0_ref[...]
        pext[0:POOL_BUF] = pool0_ref[...]
        st_re[...] = sre0_ref[...]
        st_im[...] = sim0_ref[...]

    x3 = x_ref[...]
    ms = jnp.mean(x3 * x3, axis=-1, keepdims=True)
    h3 = (x3 * lax.rsqrt(ms + EPS) * ng_ref[...]) * (1.0 + sc_ref[...]) + sh_ref[...]
    h = h3.reshape(rows, D).astype(BF16)

    def seg(lo, width):
        return _dot(h, win_ref[:, lo:lo + width])

    if flat:
        h_att = _dot(_row_order_swap(tc, nb), h).astype(BF16)
    else:
        h_att = h

    def seg_att(lo, width):
        return _dot(h_att, win_ref[:, lo:lo + width])

    def rope_table(ref):
        return jnp.tile(ref[...], (nb, 1)) if flat else jnp.broadcast_to(ref[...], (rows, HP))

    rc, rd, ru = rope_table(rc_ref), rope_table(rd_ref), rope_table(ru_ref)

    zq = seg_att(O_Q, Q_LORA)
    qa = (zq * lax.rsqrt(jnp.mean(zq * zq, axis=-1, keepdims=True) + EPS) * qag_ref[...]).astype(BF16)
    qf = _dot(qa, wq_ref[...])
    qgain = qg_ref[...] * SCALE
    heads = []
    for hd in range(HEADS):
        qh = _rope(qf[:, hd * HP:(hd + 1) * HP], rc, rd, ru)
        ss = jnp.sum(qh * qh, axis=-1, keepdims=True)
        heads.append(qh * lax.rsqrt(ss * (1.0 / QK_HEAD) + EPS) * qgain)
    emit_heads(q_ref, jnp.concatenate(heads, axis=1))

    zkv = seg_att(O_KV, KV_LORA)
    ckv = zkv * lax.rsqrt(jnp.mean(zkv * zkv, axis=-1, keepdims=True) + EPS) * kvg_ref[...]
    emit_rows(ckv_ref, ckv)
    ckv_b = ckv.astype(BF16)
    kr = _rope(seg_att(O_KR, HP), rc, rd, ru)
    emit_rows(kpe_ref, kr)
    kf = _dot(ckv_b, wk_ref[...])
    heads = []
    for hd in range(HEADS):
        kh = kf[:, hd * HP:(hd + 1) * HP] + kr
        ss = jnp.sum(kh * kh, axis=-1, keepdims=True)
        heads.append(kh * lax.rsqrt(ss * (1.0 / QK_HEAD) + EPS) * kg_ref[...])
    emit_heads(k_ref, jnp.concatenate(heads, axis=1))
    vlane = lax.broadcasted_iota(jnp.int32, (1, W), 1) % HP
    emit_heads(v_ref, _dot(ckv_b, wv_ref[...]) + jnp.where(vlane == V_HEAD, 1.0, 0.0))

    u3 = (seg(O_C, DC) * seg(O_X, DC)).reshape(tc, nb, DC)
    cext[CONV_W - 1:CONV_W - 1 + tc] = u3
    ec = cext[...]
    cw = cw_ref[...]
    y3 = cw[0:1] * ec[0:tc] + cw[1:2] * ec[1:tc + 1] + cw[2:3] * ec[2:tc + 2]
    oc_ref[...] = (seg(O_B, DC).reshape(tc, nb, DC) * y3).astype(BF16)
    tail = ec[tc:tc + CONV_W - 1]
    cext[0:CONV_W - 1] = tail
    convo_ref[...] = tail

    p3 = seg(O_P, DC).reshape(tc, nb, DC)
    pext[POOL_BUF:POOL_BUF + tc] = p3
    e0 = pext[...]
    s2 = e0[1:] + e0[:-1]
    s4 = s2[2:] + s2[:-2]
    s8 = s4[4:] + s4[:-4]
    s16 = s8[8:] + s8[:-8]
    lane = lax.broadcasted_iota(jnp.int32, (tc, nb, DC), 2)
    win = jnp.where(lane < POOL_GROUP, s2[14:14 + tc],
                    jnp.where(lane < 2 * POOL_GROUP, s4[12:12 + tc],
                              jnp.where(lane < 3 * POOL_GROUP, s8[8:8 + tc], s16)))
    dpool = (win * icnt_ref[...] - p3).reshape(rows, DC).astype(BF16)
    op_ref[...] = (_dot(dpool, pw_ref[...]) * ps_ref[...]).reshape(tc, nb, DC).astype(BF16)
    ptail = e0[tc:tc + POOL_BUF]
    pext[0:POOL_BUF] = ptail
    poolo_ref[...] = ptail

    us = seg(O_S, DC)
    ub = us.astype(BF16)
    ca, sa = 256, 1024
    for off in (0, SSM_W):
        bu[:, off:off + sa] = _dot(ub[:, 0:ca], bbd_ref[0:ca, off:off + sa])
        bu[:, off + sa:off + SSM_W] = _dot(ub[:, ca:DC], bbd_ref[ca:DC, off + sa:off + SSM_W])
    if tc == 1:
        sr, si = st_re[...], st_im[...]
        ar, ai = are_ref[...], aim_ref[...]
        nr = ar * sr - ai * si + bu[:, 0:SSM_W]
        ni = ar * si + ai * sr + bu[:, SSM_W:2 * SSM_W]
        bu[:, 0:SSM_W] = nr
        bu[:, SSM_W:2 * SSM_W] = ni
        st_re[...] = nr
        st_im[...] = ni
    else:
        cwid = 512
        for c in range(SSM_W // cwid):
            lo = c * cwid
            ar = jnp.broadcast_to(are_ref[:, lo:lo + cwid], (nb, cwid))
            ai = jnp.broadcast_to(aim_ref[:, lo:lo + cwid], (nb, cwid))

            def step(t, carry, lo=lo, ar=ar, ai=ai):
                sr, si = carry
                r0 = pl.multiple_of(t * nb, nb)
                nr = ar * sr - ai * si + bu[pl.ds(r0, nb), lo:lo + cwid]
                ni = ar * si + ai * sr + bu[pl.ds(r0, nb), SSM_W + lo:SSM_W + lo + cwid]
                bu[pl.ds(r0, nb), lo:lo + cwid] = nr
                bu[pl.ds(r0, nb), SSM_W + lo:SSM_W + lo + cwid] = ni
                return nr, ni

            sr, si = lax.fori_loop(0, tc, step, (st_re[:, lo:lo + cwid], st_im[:, lo:lo + cwid]))
            st_re[:, lo:lo + cwid] = sr
            st_im[:, lo:lo + cwid] = si
    sreo_ref[...] = st_re[...]
    simo_ref[...] = st_im[...]
    ya = (_dot(bu[:, 0:sa].astype(BF16), cbd_ref[0:sa, 0:ca])
          + _dot(bu[:, SSM_W:SSM_W + sa].astype(BF16), cbd_ref[SSM_W:SSM_W + sa, 0:ca]))
    yb = (_dot(bu[:, sa:SSM_W].astype(BF16), cbd_ref[sa:SSM_W, ca:DC])
          + _dot(bu[:, SSM_W + sa:2 * SSM_W].astype(BF16), cbd_ref[SSM_W + sa:2 * SSM_W, ca:DC]))
    ys = jnp.concatenate([ya, yb], axis=1) + sd_ref[...] * us
    zg = _dot(ys.astype(BF16), wglu_ref[...])
    os_ref[...] = (zg[:, 0:DC] * jax.nn.sigmoid(zg[:, DC:2 * DC])).reshape(tc, nb, DC).astype(BF16)

    for c in range(4):
        g = jax.nn.sigmoid(seg(O_G + c * D, D))
        gate_ref[:, :, c * D:(c + 1) * D] = g.reshape(tc, nb, D).astype(BF16)


def _mix_in(x, sc, sh, lw, tabs, states, tc, flat):
    T, nb, _ = x.shape
    rows = tc * nb
    W = HEADS * HP

    def tspec(width):
        return pl.BlockSpec((tc, nb, width), lambda i: (i, 0, 0))

    def ttab(width):
        return pl.BlockSpec((tc, 1, width), lambda i: (i, 0, 0))

    def brow(width):
        if flat:
            return jax.ShapeDtypeStruct((nb, T, width), F32), pl.BlockSpec((nb, tc, width), lambda i: (0, i, 0))
        return jax.ShapeDtypeStruct((T, nb, width), F32), tspec(width)

    const_in = [sc, sh, lw["ng1"], lw["w_in"], lw["qag"], lw["wq"], lw["qg"], lw["kvg"], lw["wk"], lw["wv"], lw["kg"]]
    tab_in = [tabs["rc"], tabs["rd"], tabs["ru"]]
    const_mid = [lw["conv_w"]]
    const_tail = [lw["pool_w"], lw["pool_scale"], lw["a_re"], lw["a_im"], lw["bbd"], lw["cbd"], lw["ssm_d"],
                  lw["w_glu"], states[0], states[1], states[2], states[3]]
    in_specs = ([tspec(D)] + [_const_spec(a.shape) for a in const_in]
                + [pl.BlockSpec((tc, HP), lambda i: (i, 0))] * 3
                + [_const_spec(a.shape) for a in const_mid] + [ttab(DC)]
                + [_const_spec(a.shape) for a in const_tail])
    if flat:
        head_shape = jax.ShapeDtypeStruct((T, nb * W), BF16)
        head_spec = pl.BlockSpec((tc, nb * W), lambda i: (i, 0))
    else:
        head_shape = jax.ShapeDtypeStruct((T, nb, W), BF16)
        head_spec = tspec(W)
    out_shape = [head_shape] * 3 + [
        brow(KV_LORA)[0], brow(HP)[0],
        jax.ShapeDtypeStruct((T, nb, 4 * D), BF16)] + [jax.ShapeDtypeStruct((T, nb, DC), BF16)] * 3 + [
        jax.ShapeDtypeStruct((CONV_W - 1, nb, DC), F32), jax.ShapeDtypeStruct((POOL_BUF, nb, DC), F32),
        jax.ShapeDtypeStruct((nb, SSM_W), F32), jax.ShapeDtypeStruct((nb, SSM_W), F32)]
    out_specs = ([head_spec] * 3 + [brow(KV_LORA)[1], brow(HP)[1], tspec(4 * D)] + [tspec(DC)] * 3
                 + [pl.BlockSpec((CONV_W - 1, nb, DC), lambda i: (0, 0, 0)),
                    pl.BlockSpec((POOL_BUF, nb, DC), lambda i: (0, 0, 0)),
                    pl.BlockSpec((nb, SSM_W), lambda i: (0, 0)), pl.BlockSpec((nb, SSM_W), lambda i: (0, 0))])
    scratch = [pltpu.VMEM((CONV_W - 1 + tc, nb, DC), F32), pltpu.VMEM((POOL_BUF + tc, nb, DC), F32),
               pltpu.VMEM((rows, 2 * SSM_W), F32), pltpu.VMEM((nb, SSM_W), F32), pltpu.VMEM((nb, SSM_W), F32)]
    return pl.pallas_call(
        functools.partial(_mix_in_kernel, flat),
        grid=(T // tc,),
        in_specs=in_specs,
        out_specs=out_specs,
        out_shape=out_shape,
        scratch_shapes=scratch,
        compiler_params=_params(("arbitrary",)),
        name="mix_in",
    )(x, *const_in, *tab_in, *const_mid, tabs["icnt"], *const_tail)


def _attn_kernel(tq, q_ref, k_ref, v_ref, o_ref):
    T = q_ref.shape[0]
    row = lax.broadcasted_iota(jnp.int32, (tq, tq), 0)
    col = lax.broadcasted_iota(jnp.int32, (tq, tq), 1)
    for i in range(T // tq):
        lo = i * tq
        q = q_ref[lo:lo + tq, :]
        s_diag = jnp.where(col <= row, _dot_nt(q, k_ref[lo:lo + tq, :]), NEG_INF)
        m = jnp.max(s_diag, axis=-1, keepdims=True)
        if i > 0:
            s_past = _dot_nt(q, k_ref[0:lo, :])
            m = jnp.maximum(m, jnp.max(s_past, axis=-1, keepdims=True))
        p_diag = jnp.exp2(s_diag - m)
        acc = _dot(p_diag.astype(BF16), v_ref[lo:lo + tq, :])
        if i > 0:
            p_past = jnp.exp2(s_past - m)
            acc = acc + _dot(p_past.astype(BF16), v_ref[0:lo, :])
        o_ref[lo:lo + tq, :] = (acc / acc[:, V_HEAD:V_HEAD + 1]).astype(o_ref.dtype)


def _prompt_attention(q, k, v, tq):
    T, BW = q.shape
    spec = pl.BlockSpec((T, HP), lambda g: (0, g))
    return pl.pallas_call(
        functools.partial(_attn_kernel, tq),
        grid=(BW // HP,),
        in_specs=[spec, spec, spec],
        out_specs=spec,
        out_shape=jax.ShapeDtypeStruct((T, BW), BF16),
        compiler_params=_params(("arbitrary",)),
        name="prompt_attention",
    )(q, k, v)


def _paged_kernel(layer, n_pages, ch, nsub, pt_ref, q_ref, kn_ref, vn_ref, kg_ref, wkt_ref, wktp_ref, wvp_ref,
                  cckv_ref, ckpe_ref, o_ref, ckv_buf, kpe_buf, sem, m_sc, l_sc, olat_sc, wk_ext, qpe_sc):
    g = pl.program_id(0)
    total = pl.num_programs(0)
    nch = n_pages // ch
    c = g % nch
    slot = g % 2
    tk = ch * PAGE
    nk = HEADS * NOPE

    def copies(step, sl):
        out = []
        for p in range(ch):
            page = pt_ref[step * ch + p]
            out.append(pltpu.make_async_copy(cckv_ref.at[layer, page], ckv_buf.at[sl, pl.ds(p * PAGE, PAGE), :],
                                             sem.at[sl, 0]))
            out.append(pltpu.make_async_copy(ckpe_ref.at[layer, page], kpe_buf.at[sl, :, pl.ds(p * PAGE, PAGE)],
                                             sem.at[sl, 1]))
        return out

    @pl.when(g == 0)
    def _():
        for cp in copies(g, slot):
            cp.start()

    @pl.when(g + 1 < total)
    def _():
        for cp in copies(g + 1, 1 - slot):
            cp.start()

    row = lax.broadcasted_iota(jnp.int32, (HEADS, HEADS * HP), 0)
    lane = lax.broadcasted_iota(jnp.int32, (HEADS, HEADS * HP), 1)
    own_head = (lane // HP) == row

    @pl.when(c == 0)
    def _():
        m_sc[...] = jnp.full_like(m_sc, NEG_INF)
        l_sc[...] = jnp.zeros_like(l_sc)
        olat_sc[...] = jnp.zeros_like(olat_sc)
        qg = q_ref[...] * kg_ref[...]
        qbd = jnp.where(own_head & ((lane % HP) < NOPE), jnp.tile(qg, (1, HEADS)), 0.0)
        qabs = _dot(qbd.astype(BF16), wktp_ref[...])
        wk_ext[0:nk, :] = wkt_ref[...]
        wk_ext[nk:nk + 2 * HEADS, :] = jnp.concatenate([qabs, jnp.zeros_like(qabs)], axis=0).astype(BF16)
        qpe_sc[...] = qg[:, NOPE:NOPE + ROPE]

    for cp in copies(g, slot):
        cp.wait()

    sub = tk // nsub
    s_parts, ck_parts = [], []
    for hf in range(nsub):
        lo = hf * sub
        ck = ckv_buf[slot, lo:lo + sub, :].astype(BF16)
        kp = kpe_buf[slot, :, lo:lo + sub]
        kn_ext = _dot_nt(wk_ext[...], ck)
        kn = kn_ext[0:nk]
        ss = jnp.sum((kn * kn).reshape(NOPE, HEADS, sub), axis=0)
        kp2 = jnp.sum(kp * kp, axis=0, keepdims=True)
        s_h = kn_ext[nk:nk + HEADS] + _dot(qpe_sc[...].astype(BF16), kp.astype(BF16))
        s_parts.append(s_h * lax.rsqrt((ss + kp2) * (1.0 / QK_HEAD) + EPS))
        ck_parts.append(ck)
    s = jnp.concatenate(s_parts, axis=1)
    m_prev = m_sc[...]
    m_new = jnp.maximum(m_prev, jnp.max(s, axis=-1, keepdims=True))
    alpha = jnp.exp2(m_prev - m_new)
    p = jnp.exp2(s - m_new)
    l_sc[...] = alpha * l_sc[...] + jnp.sum(p, axis=-1, keepdims=True)
    pv = _dot(p[:, 0:sub].astype(BF16), ck_parts[0])
    for hf in range(1, nsub):
        pv = pv + _dot(p[:, hf * sub:(hf + 1) * sub].astype(BF16), ck_parts[hf])
    olat_sc[...] = alpha * olat_sc[...] + pv
    m_sc[...] = m_new

    @pl.when(c == nch - 1)
    def _():
        s_new = jnp.sum(q_ref[...] * kn_ref[...], axis=-1, keepdims=True)
        m_prev = m_sc[...]
        m_fin = jnp.maximum(m_prev, s_new)
        alpha = jnp.exp2(m_prev - m_fin)
        p_new = jnp.exp2(s_new - m_fin)
        inv_l = 1.0 / (alpha * l_sc[...] + p_new)
        ov = _dot((alpha * olat_sc[...]).astype(BF16), wvp_ref[...])
        ov = ov + p_new * jnp.tile(vn_ref[...], (1, HEADS))
        o_ref[...] = jnp.sum(jnp.where(own_head, ov * inv_l, 0.0), axis=0, keepdims=True)


def _paged_attention(layer, q, kn, vn, kg, wkt, wktp, wvp, page_table, cache_ckv, cache_kpe, ch):
    nseq, n_pages = page_table.shape
    nch = n_pages // ch
    tk = ch * PAGE

    def per_seq(shape):
        return pl.BlockSpec((None,) + shape, lambda g, pt: (g // nch,) + (0,) * len(shape))

    def const(a):
        nd = a.ndim
        return pl.BlockSpec(a.shape, lambda g, pt: (0,) * nd, pipeline_mode=pl.Buffered(1))

    grid_spec = pltpu.PrefetchScalarGridSpec(
        num_scalar_prefetch=1,
        grid=(nseq * nch,),
        in_specs=[per_seq((HEADS, HP))] * 3 + [const(kg), const(wkt), const(wktp), const(wvp),
                                               pl.BlockSpec(memory_space=pl.ANY), pl.BlockSpec(memory_space=pl.ANY)],
        out_specs=per_seq((1, HEADS * HP)),
        scratch_shapes=[pltpu.VMEM((2, tk, KV_LORA), F32), pltpu.VMEM((2, ROPE, tk), F32),
                        pltpu.SemaphoreType.DMA((2, 2)),
                        pltpu.VMEM((HEADS, 1), F32), pltpu.VMEM((HEADS, 1), F32), pltpu.VMEM((HEADS, KV_LORA), F32),
                        pltpu.VMEM((HEADS * NOPE + 2 * HEADS, KV_LORA), BF16), pltpu.VMEM((HEADS, ROPE), F32)],
    )
    nsub = math.gcd(ch, max(1, ch // 8))
    return pl.pallas_call(
        functools.partial(_paged_kernel, layer, n_pages, ch, nsub),
        grid_spec=grid_spec,
        out_shape=jax.ShapeDtypeStruct((nseq, 1, HEADS * HP), F32),
        compiler_params=_params(("arbitrary",)),
        name="paged_attention",
    )(page_table.reshape(-1), q, kn, vn, kg, wkt, wktp, wvp, cache_ckv, cache_kpe)


def _merge_norm(flat, x_ref, gate_ref, oc_ref, om_ref, op_ref, os_ref, g1_ref, sh2_ref, sc2_ref,
                wbc_ref, wbm_ref, wbp_ref, wbs_ref, wout_ref, ng_ref):
    tt, nb = x_ref.shape[0], x_ref.shape[1]
    rows = tt * nb
    W = HEADS * HP

    def gated(o, w_ref, gi):
        return gate_ref[:, :, gi * D:(gi + 1) * D].reshape(rows, D).astype(F32) * _dot(o, w_ref[...])

    def branch(o_ref, w_ref, gi):
        return gated(o_ref[...].reshape(rows, o_ref.shape[2]), w_ref, gi)

    if flat:
        by_batch = jnp.concatenate([om_ref[:, b * W:(b + 1) * W] for b in range(nb)], axis=0)
        mla = gated(_dot(_row_order_swap(nb, tt), by_batch).astype(BF16), wbm_ref, 1)
    else:
        mla = branch(om_ref, wbm_ref, 1)
    merged = branch(oc_ref, wbc_ref, 0) + mla + branch(op_ref, wbp_ref, 2) + branch(os_ref, wbs_ref, 3)
    y = _dot(merged.astype(BF16), wout_ref[...])
    x1 = x_ref[...] + g1_ref[...] * y.reshape(tt, nb, D)
    ms = jnp.mean(x1 * x1, axis=-1, keepdims=True)
    h2 = (x1 * lax.rsqrt(ms + EPS) * ng_ref[...]) * (1.0 + sc2_ref[...]) + sh2_ref[...]
    return x1, h2.reshape(rows, D).astype(BF16)


def _mix_out_dense_kernel(flat, x_ref, gate_ref, oc_ref, om_ref, op_ref, os_ref, g1_ref, sh2_ref, sc2_ref, g2_ref,
                          wbc_ref, wbm_ref, wbp_ref, wbs_ref, wout_ref, ng_ref, wg_ref, wu_ref, wd_ref, o_ref):
    tt, nb = x_ref.shape[0], x_ref.shape[1]
    x1, h2 = _merge_norm(flat, x_ref, gate_ref, oc_ref, om_ref, op_ref, os_ref, g1_ref, sh2_ref, sc2_ref,
                         wbc_ref, wbm_ref, wbp_ref, wbs_ref, wout_ref, ng_ref)
    half = D_FF // 2
    f = None
    for c in range(2):
        a = _silu(_dot(h2, wg_ref[:, c * half:(c + 1) * half])) * _dot(h2, wu_ref[:, c * half:(c + 1) * half])
        part = _dot(a.astype(BF16), wd_ref[c * half:(c + 1) * half, :])
        f = part if f is None else f + part
    o_ref[...] = x1 + g2_ref[...] * f.reshape(tt, nb, D)


def _mix_out_router_kernel(flat, x_ref, gate_ref, oc_ref, om_ref, op_ref, os_ref, g1_ref, sh2_ref, sc2_ref,
                           wbc_ref, wbm_ref, wbp_ref, wbs_ref, wout_ref, ng_ref, rw_ref, rb_ref,
                           x1_ref, h2_ref, gates_ref):
    tt, nb = x_ref.shape[0], x_ref.shape[1]
    rows = tt * nb
    x1, h2 = _merge_norm(flat, x_ref, gate_ref, oc_ref, om_ref, op_ref, os_ref, g1_ref, sh2_ref, sc2_ref,
                         wbc_ref, wbm_ref, wbp_ref, wbs_ref, wout_ref, ng_ref)
    x1_ref[...] = x1
    h2_ref[...] = h2.reshape(tt, nb, D)
    logits = _dot(h2, rw_ref[...]) + rb_ref[...]
    lane = lax.broadcasted_iota(jnp.int32, (rows, HP), 1)
    m1 = jnp.max(logits, axis=-1, keepdims=True)
    i1 = jnp.min(jnp.where(logits == m1, lane, HP), axis=-1, keepdims=True)
    rest = jnp.where(lane == i1, -jnp.inf, logits)
    m2 = jnp.max(rest, axis=-1, keepdims=True)
    i2 = jnp.min(jnp.where(rest == m2, lane, HP), axis=-1, keepdims=True)
    e2 = jnp.exp(m2 - m1)
    w1 = 1.0 / (1.0 + e2)
    gates = jnp.where(lane == i1, w1, 0.0) + jnp.where(lane == i2, e2 * w1, 0.0)
    gates_ref[...] = gates.reshape(tt, nb, HP)


def _moe_kernel(rs, cc, h2_ref, gates_ref, x1_ref, g2_ref, wg_ref, wu_ref, wd_ref, o_ref, acc, pos_col, pos_row):
    e = pl.program_id(1)
    tt, nb = h2_ref.shape[0], h2_ref.shape[1]
    rows = tt * nb
    nsb = rows // rs

    @pl.when(e == 0)
    def _():
        acc[...] = jnp.zeros_like(acc)
        r_i = lax.broadcasted_iota(jnp.int32, (rs, rs), 0)
        c_i = lax.broadcasted_iota(jnp.int32, (rs, rs), 1)
        before = jnp.where(c_i < r_i, 1.0, 0.0).astype(BF16)
        for s in range(nsb):
            routed = gates_ref[...].reshape(rows, HP)[s * rs:(s + 1) * rs] > 0.0
            rank = _dot(before, jnp.where(routed, 1.0, 0.0).astype(BF16))
            pc = jnp.where(routed, rank, -1.0)
            pos_col[s] = pc
            pos_row[s] = pc.T

    lane = lax.broadcasted_iota(jnp.int32, (rs, HP), 1)
    for s in range(nsb):
        h2 = h2_ref[...].reshape(rows, D)[s * rs:(s + 1) * rs]
        gates = gates_ref[...].reshape(rows, HP)[s * rs:(s + 1) * rs]
        ge = jnp.sum(jnp.where(lane == e, gates, 0.0), axis=-1, keepdims=True)
        pcol = jnp.sum(jnp.where(lane == e, pos_col[s], 0.0), axis=-1, keepdims=True)
        prow = pos_row[s, pl.ds(e, 1), :]
        count = jnp.sum((prow >= 0.0).astype(jnp.int32))

        def chunk(k, carry, s=s, h2=h2, ge=ge, pcol=pcol, prow=prow):
            base = (k * cc).astype(F32)
            slot_r = lax.broadcasted_iota(jnp.int32, (cc, rs), 0).astype(F32) + base
            pick = jnp.where(slot_r == prow, 1.0, 0.0).astype(BF16)
            xg = _dot(pick, h2).astype(BF16)
            a = _silu(_dot(xg, wg_ref[...])) * _dot(xg, wu_ref[...])
            y = _dot(a.astype(BF16), wd_ref[...]).astype(BF16)
            slot_c = lax.broadcasted_iota(jnp.int32, (rs, cc), 1).astype(F32) + base
            put = jnp.where(slot_c == pcol, 1.0, 0.0).astype(BF16)
            acc[s * rs:(s + 1) * rs, :] += ge * _dot(put, y)
            return carry

        lax.fori_loop(0, (count + cc - 1) // cc, chunk, 0)

    @pl.when(e == N_EXP - 1)
    def _():
        o_ref[...] = x1_ref[...] + g2_ref[...] * acc[...].reshape(tt, nb, D)


def _mix_out(x, gates, oc, om, op, os_, mod, lw, fw, tt, moe_tt):
    T, nb, _ = x.shape

    def tspec(width):
        return pl.BlockSpec((tt, nb, width), lambda i: (i, 0, 0))

    flat = om.ndim == 2
    acts = [x, gates, oc, om, op, os_]
    act_specs = [pl.BlockSpec((tt, a.shape[1]), lambda i: (i, 0)) if a.ndim == 2 else tspec(a.shape[2])
                 for a in acts]
    g1, sh2, sc2, g2 = mod
    wts = [lw["w_br_conv"], lw["w_br_mla"], lw["w_br_pool"], lw["w_br_ssm"], lw["w_out"], lw["ng2"]]
    if "wg" in fw:
        consts = [g1, sh2, sc2, g2] + wts + [fw["wg"], fw["wu"], fw["wd"]]
        return pl.pallas_call(
            functools.partial(_mix_out_dense_kernel, flat),
            grid=(T // tt,),
            in_specs=act_specs + [_const_spec(a.shape) for a in consts],
            out_specs=tspec(D),
            out_shape=jax.ShapeDtypeStruct((T, nb, D), F32),
            compiler_params=_params(("arbitrary",)),
            name="mix_out_dense",
        )(*acts, *consts)
    consts = [g1, sh2, sc2] + wts + [fw["rw"], fw["rb"]]
    x1, h2, rg = pl.pallas_call(
        functools.partial(_mix_out_router_kernel, flat),
        grid=(T // tt,),
        in_specs=act_specs + [_const_spec(a.shape) for a in consts],
        out_specs=[tspec(D), tspec(D), tspec(HP)],
        out_shape=[jax.ShapeDtypeStruct((T, nb, D), F32), jax.ShapeDtypeStruct((T, nb, D), BF16),
                   jax.ShapeDtypeStruct((T, nb, HP), F32)],
        compiler_params=_params(("arbitrary",)),
        name="mix_out_router",
    )(*acts, *consts)
    return _moe(h2, rg, x1, g2, fw, moe_tt)


def _moe(h2, rg, x1, g2, fw, mt):
    T, nb, _ = h2.shape

    def mspec(width, **kw):
        return pl.BlockSpec((mt, nb, width), lambda i, e: (i, 0, 0), **kw)

    once = dict(pipeline_mode=pl.Buffered(1))

    def wspec(a):
        return pl.BlockSpec((None,) + a.shape[1:], lambda i, e: (e, 0, 0))

    rows = mt * nb
    rs = min(MOE_SUB_ROWS, rows)
    cc = min(MOE_CHUNK_ROWS, rs)
    return pl.pallas_call(
        functools.partial(_moe_kernel, rs, cc),
        grid=(T // mt, N_EXP),
        in_specs=[mspec(D, **once), mspec(HP, **once), mspec(D, **once), pl.BlockSpec(g2.shape, lambda i, e: (0, 0)),
                  wspec(fw["ewg"]), wspec(fw["ewu"]), wspec(fw["ewd"])],
        out_specs=mspec(D),
        out_shape=jax.ShapeDtypeStruct((T, nb, D), F32),
        scratch_shapes=[pltpu.VMEM((rows, D), F32), pltpu.VMEM((rows // rs, rs, HP), F32),
                        pltpu.VMEM((rows // rs, HP, rs), F32)],
        compiler_params=_params(("arbitrary", "arbitrary")),
        name="moe",
    )(h2, rg, x1, g2, fw["ewg"], fw["ewu"], fw["ewd"])


def _pad_heads(w, width):
    k = w.shape[0]
    return jnp.pad(w.reshape(k, HEADS, width), ((0, 0), (0, 0), (0, HP - width))).reshape(k, HEADS * HP)


def _layer_weights(l, W):
    w_in = W["w_in"][l]
    pts = [0]
    for s in (Q_LORA, KV_LORA, ROPE, DC, DC, DC, DC, DC, 4 * D):
        pts.append(pts[-1] + s)
    zq, zkv, zkr, zb, zc, zx, zp, zs, zg = (w_in[:, pts[i]:pts[i + 1]] for i in range(9))
    zkr = jnp.pad(zkr, ((0, 0), (NOPE, HP - NOPE - ROPE)))
    w_in_p = jnp.concatenate([zq, zkv, zkr, zb, zc, zx, zp, zs, zg], axis=1).astype(BF16)
    kv = W["w_kv_up"][l].reshape(KV_LORA, HEADS, NOPE + V_HEAD)
    wk = kv[:, :, :NOPE].reshape(KV_LORA, HEADS * NOPE)
    wv = kv[:, :, NOPE:].reshape(KV_LORA, HEADS * V_HEAD)
    pad_g = lambda g: jnp.pad(g, (0, HP - QK_HEAD)).reshape(1, HP)
    pw = jnp.einsum("gij,gh->gihj", W["pool_w"][l], jnp.eye(4, dtype=F32)).reshape(DC, DC)
    dt = jnp.exp(W["ssm_log_dt"][l])[:, None]
    ar = jnp.minimum(W["ssm_a_re"][l], -1e-4)
    ai = W["ssm_a_im"][l]
    mag = jnp.exp(dt * ar)
    ab_re, ab_im = mag * jnp.cos(dt * ai), mag * jnp.sin(dt * ai)
    den = ar * ar + ai * ai
    nr, ni = ab_re - 1.0, ab_im
    k_re, k_im = (nr * ar + ni * ai) / den, (ni * ar - nr * ai) / den
    br, bi = W["ssm_b_re"][l], W["ssm_b_im"][l]
    bb_re = k_re[..., None] * br - k_im[..., None] * bi
    bb_im = k_re[..., None] * bi + k_im[..., None] * br
    eye_g = jnp.eye(SSM_G, dtype=F32)
    to_bd = lambda b: jnp.einsum("gpn,gh->gnhp", b, eye_g).reshape(DC, SSM_W)
    bbd = jnp.concatenate([to_bd(bb_re), to_bd(bb_im)], axis=1)
    from_bd = lambda c: jnp.einsum("gnp,gh->gphn", c, eye_g).reshape(SSM_W, DC)
    cbd = jnp.concatenate([from_bd(W["ssm_c_re"][l]), -from_bd(W["ssm_c_im"][l])], axis=0)
    return dict(
        ng1=W["norm_mix_g"][l].reshape(1, D), ng2=W["norm_ffn_g"][l].reshape(1, D), w_in=w_in_p,
        qag=W["q_a_norm_g"][l].reshape(1, Q_LORA), wq=_pad_heads(W["w_q_up"][l], QK_HEAD).astype(BF16),
        qg=pad_g(W["q_norm_g"][l]), kvg=W["kv_a_norm_g"][l].reshape(1, KV_LORA),
        wk=_pad_heads(wk, NOPE).astype(BF16), wv=_pad_heads(wv, V_HEAD).astype(BF16), kg=pad_g(W["k_norm_g"][l]),
        wkt=kv[:, :, :NOPE].transpose(2, 1, 0).reshape(HEADS * NOPE, KV_LORA).astype(BF16),
        wktp=_pad_heads(wk, NOPE).T.astype(BF16),
        conv_w=W["conv_w"][l], pool_w=pw.astype(BF16), pool_scale=W["pool_scale"][l].reshape(1, DC),
        a_re=ab_re.reshape(1, SSM_W), a_im=ab_im.reshape(1, SSM_W), bbd=bbd.astype(BF16), cbd=cbd.astype(BF16),
        ssm_d=W["ssm_d"][l].reshape(1, DC), w_glu=W["ssm_w_glu"][l].astype(BF16),
        w_br_conv=W["w_br_conv"][l].astype(BF16), w_br_pool=W["w_br_pool"][l].astype(BF16),
        w_br_ssm=W["w_br_ssm"][l].astype(BF16),
        w_br_mla=jnp.pad(W["w_br_mla"][l].reshape(HEADS, V_HEAD, D), ((0, 0), (0, HP - V_HEAD), (0, 0)))
        .reshape(HEADS * HP, D).astype(BF16),
        w_out=W["w_out"][l].astype(BF16),
    )


def _ffn_weights(l, W):
    if l % 2 == 0:
        return dict(wg=W["ffn_w_gate"][l // 2].astype(BF16), wu=W["ffn_w_up"][l // 2].astype(BF16),
                    wd=W["ffn_w_down"][l // 2].astype(BF16))
    rw = jnp.pad(W["moe_router_w"][l // 2], ((0, 0), (0, HP - N_EXP))).astype(BF16)
    rb = jnp.pad(W["moe_router_b"][l // 2], (0, HP - N_EXP), constant_values=NEG_INF).reshape(1, HP)
    return dict(rw=rw, rb=rb, ewg=W["moe_w_gate"][l // 2].astype(BF16), ewu=W["moe_w_up"][l // 2].astype(BF16),
                ewd=W["moe_w_down"][l // 2].astype(BF16))


def _position_tables(pos):
    inv_freq = jnp.power(ROPE_BASE, -jnp.arange(0, ROPE, 2, dtype=F32) / ROPE)
    ang = pos.astype(F32)[:, None] * inv_freq[None, :]
    cos, sin = jnp.cos(ang), jnp.sin(ang)
    half = ROPE // 2
    T = pos.shape[0]
    one = jnp.ones((T, NOPE), F32)
    zero = lambda n: jnp.zeros((T, n), F32)
    rc = jnp.concatenate([one, cos, cos, jnp.ones((T, HP - QK_HEAD), F32)], axis=1)
    rd = jnp.concatenate([zero(NOPE), -sin, zero(HP - NOPE - half)], axis=1)
    ru = jnp.concatenate([zero(NOPE + half), sin, zero(HP - QK_HEAD)], axis=1)
    cnt = jnp.concatenate([jnp.broadcast_to(jnp.minimum(pos + 1, w).astype(F32)[:, None], (T, POOL_GROUP))
                           for w in POOL_WINDOWS], axis=1)
    return dict(rc=rc, rd=rd, ru=ru, icnt=(1.0 / cnt)[:, None])


def _trunk(x, mod, pos0, states, attend, LW, FW, tc, tt, moe_tt, flat):
    T, nb, _ = x.shape
    tabs = _position_tables(pos0 + jnp.arange(T))
    outs = []
    for l in range(DEPTH):
        sh1, sc1, g1, sh2, sc2, g2 = (mod[l, j] for j in range(6))
        q, k, v, ckv, kpe, gates, oc, op, os_, conv_n, pool_n, sre_n, sim_n = _mix_in(
            x, sc1, sh1, LW[l], tabs, states[l], tc, flat)
        om = attend(l, q, k, v)
        x = _mix_out(x, gates, oc, om, op, os_, (g1, sh2, sc2, g2), LW[l], FW[l], tt, moe_tt)
        if not flat:
            ckv, kpe = jnp.swapaxes(ckv, 0, 1), jnp.swapaxes(kpe, 0, 1)
        outs.append((ckv, kpe[:, :, NOPE:NOPE + ROPE], conv_n, pool_n, sre_n, sim_n))
    return x, outs


def kernel(x_prompt, x_sample, c_prompt, c_sample, cache_ckv, cache_kpe, page_table, state_conv, state_pool, state_ssm_re, state_ssm_im, ada_w, ada_b, norm_mix_g, norm_ffn_g, w_in, q_a_norm_g, w_q_up, kv_a_norm_g, w_kv_up, q_norm_g, k_norm_g, conv_w, pool_w, pool_scale, ssm_a_re, ssm_a_im, ssm_b_re, ssm_b_im, ssm_c_re, ssm_c_im, ssm_d, ssm_log_dt, ssm_w_glu, w_br_conv, w_br_mla, w_br_pool, w_br_ssm, w_out, ffn_w_gate, ffn_w_up, ffn_w_down, moe_router_w, moe_router_b, moe_w_gate, moe_w_up, moe_w_down):
    W = dict(norm_mix_g=norm_mix_g, norm_ffn_g=norm_ffn_g, w_in=w_in, q_a_norm_g=q_a_norm_g, w_q_up=w_q_up,
             kv_a_norm_g=kv_a_norm_g, w_kv_up=w_kv_up, q_norm_g=q_norm_g, k_norm_g=k_norm_g, conv_w=conv_w,
             pool_w=pool_w, pool_scale=pool_scale, ssm_a_re=ssm_a_re, ssm_a_im=ssm_a_im, ssm_b_re=ssm_b_re,
             ssm_b_im=ssm_b_im, ssm_c_re=ssm_c_re, ssm_c_im=ssm_c_im, ssm_d=ssm_d, ssm_log_dt=ssm_log_dt,
             ssm_w_glu=ssm_w_glu, w_br_conv=w_br_conv, w_br_mla=w_br_mla, w_br_pool=w_br_pool, w_br_ssm=w_br_ssm,
             w_out=w_out, ffn_w_gate=ffn_w_gate, ffn_w_up=ffn_w_up, ffn_w_down=ffn_w_down,
             moe_router_w=moe_router_w, moe_router_b=moe_router_b, moe_w_gate=moe_w_gate, moe_w_up=moe_w_up,
             moe_w_down=moe_w_down)
    LW = [_layer_weights(l, W) for l in range(DEPTH)]
    FW = [_ffn_weights(l, W) for l in range(DEPTH)]
    B, T, _ = x_prompt.shape
    nseq = x_sample.shape[0]
    n_pages = page_table.shape[1]
    past = n_pages * PAGE

    mod = _ada(jnp.concatenate([c_prompt, c_sample], axis=0), ada_w, ada_b)
    mod_p, mod_s = mod[:, :, :B], mod[:, :, B:]

    tc = min(16, T)
    tq = min(256, T)
    zero_states = [(jnp.zeros((CONV_W - 1, B, DC), F32), jnp.zeros((POOL_BUF, B, DC), F32),
                    jnp.zeros((B, SSM_W), F32), jnp.zeros((B, SSM_W), F32))] * DEPTH

    def prompt_attend(l, q, k, v):
        return _prompt_attention(q, k, v, tq)

    xp, outs_p = _trunk(jnp.swapaxes(x_prompt, 0, 1), mod_p, 0, zero_states, prompt_attend, LW, FW,
                        tc, min(16, T), min(64, T), True)
    y_prompt = jnp.swapaxes(xp, 0, 1)

    states_s = [(jnp.swapaxes(state_conv[l], 0, 1), jnp.swapaxes(state_pool[l], 0, 1),
                 state_ssm_re[l].reshape(nseq, SSM_W), state_ssm_im[l].reshape(nseq, SSM_W)) for l in range(DEPTH)]
    ch = math.gcd(32, n_pages)
    cache_kpe_t = jnp.swapaxes(cache_kpe, 2, 3)

    def sample_attend(l, q, k, v):
        heads = lambda a: a.reshape(nseq, HEADS, HP).astype(F32)
        o = _paged_attention(l, heads(q), heads(k), heads(v), LW[l]["kg"], LW[l]["wkt"], LW[l]["wktp"],
                             LW[l]["wv"], page_table, cache_ckv, cache_kpe_t, ch)
        return o.reshape(1, nseq, HEADS * HP).astype(BF16)

    xs, outs_s = _trunk(jnp.swapaxes(x_sample, 0, 1), mod_s, past, states_s, sample_attend, LW, FW, 1, 1, 1,
                        False)
    y_sample = jnp.swapaxes(xs, 0, 1)

    def gather(outs, n):
        ckv = jnp.stack([o[0] for o in outs])
        kpe = jnp.stack([o[1] for o in outs])
        conv = jnp.stack([jnp.swapaxes(o[2], 0, 1) for o in outs])
        pool = jnp.stack([jnp.swapaxes(o[3], 0, 1) for o in outs])
        sre = jnp.stack([o[4].reshape(n, SSM_G, SSM_P) for o in outs])
        sim = jnp.stack([o[5].reshape(n, SSM_G, SSM_P) for o in outs])
        return ckv, kpe, conv, pool, sre, sim

    return (y_prompt, y_sample) + gather(outs_p, B) + gather(outs_s, nseq)
```
